```python
import math
import jax
import jax.numpy as jnp
from jax import lax
import numpy as np

D_MODEL = 2048
BATCH = 16
SEQ = 2048
DEPTH = 4

N_MIXERS = 2
HEAD_DIM = 128
FOX_HEADS = D_MODEL // HEAD_DIM
FOX_WIDTH = FOX_HEADS * HEAD_DIM
DIFF_HEADS = D_MODEL // (2 * HEAD_DIM)
DIFF_WIDTH = DIFF_HEADS * 2 * HEAD_DIM
ROT_DIM = HEAD_DIM // 4
ROPE_THETA = 500000.0
Q_BLOCK = 128
N_EXPERTS = 64
TOP_K = 8
N_GROUPS = 8
TOPK_GROUPS = 4
EXPERTS_PER_GROUP = N_EXPERTS // N_GROUPS
EXPERT_DIM = D_MODEL // 4
SHARED_DIM = EXPERT_DIM
ROUTED_SCALE = 2.5
ROW_BLOCK = 128
DEEPNORM_ALPHA = (2 * DEPTH) ** 0.25
DEEPNORM_BETA = (8 * DEPTH) ** -0.25
LN_EPS = 1e-5
SUBLN_EPS = 1e-5

kernel_name = "hybrid_fox_diffattn_moe_deepnorm"


def rotary_tables(seq):
    inv_freq = ROPE_THETA ** (-jnp.arange(0, ROT_DIM, 2, dtype=jnp.float32) / ROT_DIM)
    ang = jnp.arange(seq, dtype=jnp.float32)[:, None] * inv_freq[None, :]
    return jnp.cos(ang), jnp.sin(ang)


def apply_partial_rotary(x, cos, sin):
    half = ROT_DIM // 2
    shape = (cos.shape[0],) + (1,) * (x.ndim - 3) + (half,)
    c = cos.reshape(shape).astype(x.dtype)
    s = sin.reshape(shape).astype(x.dtype)
    x1, x2, rest = x[..., :half], x[..., half:ROT_DIM], x[..., ROT_DIM:]
    return jnp.concatenate([x1 * c - x2 * s, x2 * c + x1 * s, rest], axis=-1)


def causal_block_attention(q, k, v, log_f_cum=None):
    b, h, s, dk = q.shape
    dv = v.shape[-1]
    n_blocks = s // Q_BLOCK
    scale = dk ** -0.5
    key_pos = jnp.arange(s)
    xs = (jnp.arange(n_blocks), jnp.moveaxis(q.reshape(b, h, n_blocks, Q_BLOCK, dk), 2, 0))
    if log_f_cum is not None:
        xs = xs + (jnp.moveaxis(log_f_cum.reshape(b, h, n_blocks, Q_BLOCK), 2, 0),)

    def attend_block(xs_i):
        i, q_i = xs_i[0], xs_i[1]
        logits = jnp.einsum("bhqd,bhkd->bhqk", q_i, k, preferred_element_type=jnp.float32) * scale
        if log_f_cum is not None:
            logits = logits + (xs_i[2][..., :, None] - log_f_cum[:, :, None, :])
        query_pos = i * Q_BLOCK + jnp.arange(Q_BLOCK)
        logits = jnp.where(key_pos[None, :] <= query_pos[:, None], logits, -jnp.inf)
        probs = jax.nn.softmax(logits, axis=-1)
        return jnp.einsum("bhqk,bhkd->bhqd", probs.astype(v.dtype), v)

    out = lax.map(attend_block, xs)
    return jnp.moveaxis(out, 0, 2).reshape(b, h, s, dv)


def forgetting_attention(x, w_in, b_forget, w_out):
    b, s, _ = x.shape
    proj = x @ w_in
    def heads(t):
        return t.reshape(b, s, FOX_HEADS, HEAD_DIM).transpose(0, 2, 1, 3)
    q = heads(proj[..., :FOX_WIDTH])
    k = heads(proj[..., FOX_WIDTH:2 * FOX_WIDTH])
    v = heads(proj[..., 2 * FOX_WIDTH:3 * FOX_WIDTH])
    f_logit = (proj[..., 3 * FOX_WIDTH:] + b_forget).astype(jnp.float32)
    log_f_cum = jnp.cumsum(jax.nn.log_sigmoid(f_logit), axis=1).transpose(0, 2, 1)
    o = causal_block_attention(q, k, v, log_f_cum)
    return o.transpose(0, 2, 1, 3).reshape(b, s, FOX_WIDTH) @ w_out


def differential_attention(x, layer_idx, cos, sin, w_in, lambda_qk, subln_g, w_out):
    b, s, _ = x.shape
    proj = x @ w_in
    q = proj[..., :DIFF_WIDTH].reshape(b, s, DIFF_HEADS, 2, HEAD_DIM)
    k = proj[..., DIFF_WIDTH:2 * DIFF_WIDTH].reshape(b, s, DIFF_HEADS, 2, HEAD_DIM)
    v = proj[..., 2 * DIFF_WIDTH:].reshape(b, s, DIFF_HEADS, 2 * HEAD_DIM).transpose(0, 2, 1, 3)
    q = apply_partial_rotary(q, cos, sin).transpose(3, 0, 2, 1, 4)
    k = apply_partial_rotary(k, cos, sin).transpose(3, 0, 2, 1, 4)
    a1 = causal_block_attention(q[0], k[0], v)
    a2 = causal_block_attention(q[1], k[1], v)
    lam_init = 0.8 - 0.6 * math.exp(-0.3 * layer_idx)
    lq = lambda_qk.astype(jnp.float32)
    lam = jnp.exp(jnp.sum(lq[0] * lq[1])) - jnp.exp(jnp.sum(lq[2] * lq[3])) + lam_init
    o = a1.astype(jnp.float32) - lam * a2.astype(jnp.float32)
    o = o * lax.rsqrt(jnp.mean(jnp.square(o), axis=-1, keepdims=True) + SUBLN_EPS)
    o = o * subln_g.astype(jnp.float32) * (1.0 - lam_init)
    o = o.astype(x.dtype).transpose(0, 2, 1, 3).reshape(b, s, DIFF_WIDTH)
    return o @ w_out


def moe_ffn(x, w_router, b_router, w_gate_up, w_down, ws_gate_up, ws_down):
    b, s, d = x.shape
    n_tok = b * s
    xt = x.reshape(n_tok, d)
    scores = jax.nn.sigmoid(jnp.dot(xt, w_router, preferred_element_type=jnp.float32))
    biased = scores + b_router.astype(jnp.float32)
    group_scores = lax.top_k(biased.reshape(n_tok, N_GROUPS, EXPERTS_PER_GROUP), 2)[0].sum(-1)
    _, top_groups = lax.top_k(group_scores, TOPK_GROUPS)
    group_mask = jnp.any(top_groups[:, :, None] == jnp.arange(N_GROUPS)[None, None, :], axis=1)
    expert_mask = jnp.repeat(group_mask, EXPERTS_PER_GROUP, axis=1)
    _, top_idx = lax.top_k(jnp.where(expert_mask, biased, -jnp.inf), TOP_K)
    top_w = jnp.take_along_axis(scores, top_idx, axis=1)
    top_w = top_w / jnp.sum(top_w, axis=-1, keepdims=True) * ROUTED_SCALE

    n_assign = n_tok * TOP_K
    expert_flat = top_idx.reshape(n_assign).astype(jnp.int32)
    token_flat = jnp.arange(n_assign, dtype=jnp.int32) // TOP_K
    weight_flat = top_w.reshape(n_assign)
    order = jnp.argsort(expert_flat)
    expert_sorted = expert_flat[order]
    counts = jnp.bincount(expert_flat, length=N_EXPERTS)
    padded = (counts + ROW_BLOCK - 1) // ROW_BLOCK * ROW_BLOCK
    start = jnp.cumsum(counts) - counts
    padded_end = jnp.cumsum(padded)
    padded_start = padded_end - padded
    dest = padded_start[expert_sorted] + (jnp.arange(n_assign, dtype=jnp.int32) - start[expert_sorted])
    n_blocks = -(-n_assign // ROW_BLOCK) + N_EXPERTS
    n_rows = n_blocks * ROW_BLOCK
    row_token = jnp.zeros((n_rows,), jnp.int32).at[dest].set(token_flat[order])
    row_weight = jnp.zeros((n_rows,), jnp.float32).at[dest].set(weight_flat[order])
    block_expert = jnp.minimum(
        jnp.searchsorted(padded_end, jnp.arange(n_blocks, dtype=jnp.int32) * ROW_BLOCK, side="right"),
        N_EXPERTS - 1)

    def expert_block(acc, xs):
        e, toks, wts = xs
        xb = xt[toks]
        gate, up = jnp.split(xb @ w_gate_up[e], 2, axis=-1)
        yb = (jax.nn.silu(gate) * up) @ w_down[e]
        return acc.at[toks].add(yb.astype(jnp.float32) * wts[:, None]), None

    routed, _ = lax.scan(expert_block, jnp.zeros((n_tok, d), jnp.float32),
                         (block_expert, row_token.reshape(n_blocks, ROW_BLOCK),
                          row_weight.reshape(n_blocks, ROW_BLOCK)))
    gate_s, up_s = jnp.split(xt @ ws_gate_up, 2, axis=-1)
    shared = (jax.nn.silu(gate_s) * up_s) @ ws_down
    return (shared.astype(jnp.float32) + routed).astype(x.dtype).reshape(b, s, d)


def deepnorm_residual(x, y, g, b):
    z = (DEEPNORM_ALPHA * x + y).astype(jnp.float32)
    mu = jnp.mean(z, axis=-1, keepdims=True)
    var = jnp.mean(jnp.square(z - mu), axis=-1, keepdims=True)
    return ((z - mu) * lax.rsqrt(var + LN_EPS) * g + b).astype(x.dtype)


def _layer_params(key, i):
    ks = jax.random.split(key, 12)
    nrm = jax.random.normal
    s_in = D_MODEL ** -0.5
    p = {}
    if i % N_MIXERS == 0:
        w_qk = nrm(ks[0], (D_MODEL, 2 * FOX_WIDTH), jnp.float32) * s_in
        w_v = nrm(ks[1], (D_MODEL, FOX_WIDTH), jnp.float32) * s_in * DEEPNORM_BETA
        w_f = nrm(ks[2], (D_MODEL, FOX_HEADS), jnp.float32) * s_in
        p[f"fox_w_in_{i}"] = jnp.concatenate([w_qk, w_v, w_f], axis=1)
        p[f"fox_b_forget_{i}"] = jax.random.uniform(ks[3], (FOX_HEADS,), jnp.float32, 1.0, 4.0)
        p[f"fox_w_out_{i}"] = nrm(ks[4], (FOX_WIDTH, D_MODEL), jnp.float32) * FOX_WIDTH ** -0.5 * DEEPNORM_BETA
    else:
        w_qk = nrm(ks[0], (D_MODEL, 2 * DIFF_WIDTH), jnp.float32) * s_in
        w_v = nrm(ks[1], (D_MODEL, DIFF_WIDTH), jnp.float32) * s_in * DEEPNORM_BETA
        p[f"diff_w_in_{i}"] = jnp.concatenate([w_qk, w_v], axis=1)
        p[f"diff_lambda_qk_{i}"] = nrm(ks[2], (4, HEAD_DIM), jnp.float32) * 0.1
        p[f"diff_subln_g_{i}"] = 1.0 + 0.02 * nrm(ks[3], (2 * HEAD_DIM,), jnp.float32)
        p[f"diff_w_out_{i}"] = nrm(ks[4], (DIFF_WIDTH, D_MODEL), jnp.float32) * DIFF_WIDTH ** -0.5 * DEEPNORM_BETA
    p[f"ln_mix_g_{i}"] = 1.0 + 0.02 * nrm(ks[5], (D_MODEL,), jnp.float32)
    p[f"ln_mix_b_{i}"] = 0.02 * nrm(ks[6], (D_MODEL,), jnp.float32)
    kr = jax.random.split(ks[7], 6)
    p[f"moe_w_router_{i}"] = nrm(kr[0], (D_MODEL, N_EXPERTS), jnp.float32) * s_in
    p[f"moe_b_router_{i}"] = 0.01 * nrm(kr[1], (N_EXPERTS,), jnp.float32)
    p[f"moe_w_gate_up_{i}"] = nrm(kr[2], (N_EXPERTS, D_MODEL, 2 * EXPERT_DIM), jnp.float32) * s_in
    p[f"moe_w_down_{i}"] = nrm(kr[3], (N_EXPERTS, EXPERT_DIM, D_MODEL), jnp.float32) * EXPERT_DIM ** -0.5 * DEEPNORM_BETA
    p[f"moe_ws_gate_up_{i}"] = nrm(kr[4], (D_MODEL, 2 * SHARED_DIM), jnp.float32) * s_in
    p[f"moe_ws_down_{i}"] = nrm(kr[5], (SHARED_DIM, D_MODEL), jnp.float32) * SHARED_DIM ** -0.5 * DEEPNORM_BETA
    p[f"ln_ffn_g_{i}"] = 1.0 + 0.02 * nrm(ks[8], (D_MODEL,), jnp.float32)
    p[f"ln_ffn_b_{i}"] = 0.02 * nrm(ks[9], (D_MODEL,), jnp.float32)
    return p


def setup_inputs(seed: int = 0) -> dict:
    key = jax.random.key(seed)
    keys = jax.random.split(key, 1 + DEPTH)
    inputs = {"x": jax.random.normal(keys[0], (BATCH, SEQ, D_MODEL), jnp.float32)}
    for i in range(DEPTH):
        inputs.update(_layer_params(keys[1 + i], i))
    return inputs


def reference(x,
              fox_w_in_0, fox_b_forget_0, fox_w_out_0, ln_mix_g_0, ln_mix_b_0,
              moe_w_router_0, moe_b_router_0, moe_w_gate_up_0, moe_w_down_0,
              moe_ws_gate_up_0, moe_ws_down_0, ln_ffn_g_0, ln_ffn_b_0,
              diff_w_in_1, diff_lambda_qk_1, diff_subln_g_1, diff_w_out_1, ln_mix_g_1, ln_mix_b_1,
              moe_w_router_1, moe_b_router_1, moe_w_gate_up_1, moe_w_down_1,
              moe_ws_gate_up_1, moe_ws_down_1, ln_ffn_g_1, ln_ffn_b_1,
              fox_w_in_2, fox_b_forget_2, fox_w_out_2, ln_mix_g_2, ln_mix_b_2,
              moe_w_router_2, moe_b_router_2, moe_w_gate_up_2, moe_w_down_2,
              moe_ws_gate_up_2, moe_ws_down_2, ln_ffn_g_2, ln_ffn_b_2,
              diff_w_in_3, diff_lambda_qk_3, diff_subln_g_3, diff_w_out_3, ln_mix_g_3, ln_mix_b_3,
              moe_w_router_3, moe_b_router_3, moe_w_gate_up_3, moe_w_down_3,
              moe_ws_gate_up_3, moe_ws_down_3, ln_ffn_g_3, ln_ffn_b_3):
    mix_params = [
        (fox_w_in_0, fox_b_forget_0, fox_w_out_0),
        (diff_w_in_1, diff_lambda_qk_1, diff_subln_g_1, diff_w_out_1),
        (fox_w_in_2, fox_b_forget_2, fox_w_out_2),
        (diff_w_in_3, diff_lambda_qk_3, diff_subln_g_3, diff_w_out_3),
    ]
    norm_mix = [(ln_mix_g_0, ln_mix_b_0), (ln_mix_g_1, ln_mix_b_1),
                (ln_mix_g_2, ln_mix_b_2), (ln_mix_g_3, ln_mix_b_3)]
    moe_params = [
        (moe_w_router_0, moe_b_router_0, moe_w_gate_up_0, moe_w_down_0, moe_ws_gate_up_0, moe_ws_down_0),
        (moe_w_router_1, moe_b_router_1, moe_w_gate_up_1, moe_w_down_1, moe_ws_gate_up_1, moe_ws_down_1),
        (moe_w_router_2, moe_b_router_2, moe_w_gate_up_2, moe_w_down_2, moe_ws_gate_up_2, moe_ws_down_2),
        (moe_w_router_3, moe_b_router_3, moe_w_gate_up_3, moe_w_down_3, moe_ws_gate_up_3, moe_ws_down_3),
    ]
    norm_ffn = [(ln_ffn_g_0, ln_ffn_b_0), (ln_ffn_g_1, ln_ffn_b_1),
                (ln_ffn_g_2, ln_ffn_b_2), (ln_ffn_g_3, ln_ffn_b_3)]
    cos, sin = rotary_tables(x.shape[1])
    h = x
    for i in range(DEPTH):
        if i % N_MIXERS == 0:
            y = forgetting_attention(h, *mix_params[i])
        else:
            y = differential_attention(h, i, cos, sin, *mix_params[i])
        h = deepnorm_residual(h, y, *norm_mix[i])
        h = deepnorm_residual(h, moe_ffn(h, *moe_params[i]), *norm_ffn[i])
    return h
```

```python
import functools
import math

import jax
import jax.numpy as jnp
from jax import lax
from jax.experimental import pallas as pl
from jax.experimental.pallas import tpu as pltpu

D_MODEL = 2048
DEPTH = 4
HEAD_DIM = 128
LANES = 128
ROW_CHUNKS = D_MODEL // LANES
FOX_HEADS = D_MODEL // HEAD_DIM
DIFF_HEADS = D_MODEL // (2 * HEAD_DIM)
ROT_DIM = HEAD_DIM // 4
ROPE_THETA = 500000.0
N_EXPERTS = 64
TOP_K = 8
N_GROUPS = 8
TOPK_GROUPS = 4
EXPERTS_PER_GROUP = N_EXPERTS // N_GROUPS
EXPERT_DIM = D_MODEL // 4
ROUTED_SCALE = 2.5
DEEPNORM_ALPHA = (2 * DEPTH) ** 0.25
LN_EPS = 1e-5
SUBLN_EPS = 1e-5

ATT_BLOCK = 256
EXPERT_BLOCK = 256
COPY_GROUP = 1024
VMEM_LIMIT = 56 * 1024 * 1024

_HIGHEST = lax.Precision.HIGHEST
_NT = (((1,), (1,)), ((), ()))


def _params(*sem):
    return pltpu.CompilerParams(dimension_semantics=sem, vmem_limit_bytes=VMEM_LIMIT)


def _mm_kernel(a_ref, b_ref, s_ref, o_ref):
    acc = jnp.dot(a_ref[...], b_ref[...], preferred_element_type=jnp.float32)
    o_ref[...] = (acc * s_ref[...]).astype(o_ref.dtype)


def _matmul(a, b, col_scale, out_dtype, tm, tn, name):
    m, k = a.shape
    n = b.shape[1]
    return pl.pallas_call(
        _mm_kernel,
        grid=(m // tm, n // tn),
        in_specs=[pl.BlockSpec((tm, k), lambda i, j: (i, 0)),
                  pl.BlockSpec((k, tn), lambda i, j: (0, j)),
                  pl.BlockSpec((1, tn), lambda i, j: (0, j))],
        out_specs=pl.BlockSpec((tm, tn), lambda i, j: (i, j)),
        out_shape=jax.ShapeDtypeStruct((m, n), out_dtype),
        compiler_params=_params("parallel", "parallel"),
        name=name,
    )(a, b, col_scale)


def _fgate_kernel(fl_ref, b_ref, c_ref, ct_ref, *, seq, chunk):
    x = fl_ref[...] + b_ref[...]
    ls = -(jnp.maximum(-x, 0.0) + jnp.log1p(jnp.exp(-jnp.abs(x))))
    r = lax.broadcasted_iota(jnp.int32, (chunk, chunk), 0)
    c = lax.broadcasted_iota(jnp.int32, (chunk, chunk), 1)
    tri = (r >= c).astype(jnp.float32)
    carry = jnp.zeros((1, LANES), jnp.float32)
    for ch in range(seq // chunk):
        blk = jnp.dot(tri, ls[ch * chunk:(ch + 1) * chunk], precision=_HIGHEST,
                      preferred_element_type=jnp.float32) + carry
        carry = blk[chunk - 1:chunk, :]
        c_ref[0, ch * chunk:(ch + 1) * chunk, :] = blk
        ct_ref[0, :, ch * chunk:(ch + 1) * chunk] = blk.T[:FOX_HEADS]


def _forget_cumsum(f_logit, b_pad, batch, seq):
    chunk = 256
    return pl.pallas_call(
        functools.partial(_fgate_kernel, seq=seq, chunk=chunk),
        grid=(batch,),
        in_specs=[pl.BlockSpec((seq, LANES), lambda b: (b, 0)),
                  pl.BlockSpec((1, LANES), lambda b: (0, 0))],
        out_specs=[pl.BlockSpec((1, seq, LANES), lambda b: (b, 0, 0)),
                   pl.BlockSpec((1, FOX_HEADS, seq), lambda b: (b, 0, 0))],
        out_shape=[jax.ShapeDtypeStruct((batch, seq, LANES), jnp.float32),
                   jax.ShapeDtypeStruct((batch, FOX_HEADS, seq), jnp.float32)],
        compiler_params=_params("parallel"),
        name="forget_cumsum",
    )(f_logit, b_pad)


def _causal_attend(q, k_ref, v_ref, i, row_bias, col_bias):
    blk = ATT_BLOCK
    lo = i * blk
    s_d = lax.dot_general(q, k_ref[lo:lo + blk, :], _NT, preferred_element_type=jnp.float32)
    if row_bias is not None:
        s_d = s_d + row_bias - col_bias[:, lo:lo + blk]
    r = lax.broadcasted_iota(jnp.int32, (blk, blk), 0)
    c = lax.broadcasted_iota(jnp.int32, (blk, blk), 1)
    s_d = jnp.where(c <= r, s_d, -jnp.inf)
    m = jnp.max(s_d, axis=-1, keepdims=True)
    if i > 0:
        s_o = lax.dot_general(q, k_ref[0:lo, :], _NT, preferred_element_type=jnp.float32)
        if row_bias is not None:
            s_o = s_o + row_bias - col_bias[:, 0:lo]
        m = jnp.maximum(m, jnp.max(s_o, axis=-1, keepdims=True))
    p_d = jnp.exp(s_d - m)
    l = jnp.sum(p_d, axis=-1, keepdims=True)
    acc = jnp.dot(p_d.astype(jnp.bfloat16), v_ref[lo:lo + blk, :], preferred_element_type=jnp.float32)
    if i > 0:
        p_o = jnp.exp(s_o - m)
        l = l + jnp.sum(p_o, axis=-1, keepdims=True)
        acc = acc + jnp.dot(p_o.astype(jnp.bfloat16), v_ref[0:lo, :], preferred_element_type=jnp.float32)
    return acc / l


def _fox_attn_kernel(q_ref, k_ref, v_ref, c_ref, ct_ref, o_ref, *, seq):
    h = pl.program_id(1)
    c_all = c_ref[0]
    lane = lax.broadcasted_iota(jnp.int32, c_all.shape, 1)
    c_row = jnp.sum(jnp.where(lane == h, c_all, 0.0), axis=1, keepdims=True)
    ct_all = ct_ref[0]
    sub = lax.broadcasted_iota(jnp.int32, ct_all.shape, 0)
    c_col = jnp.sum(jnp.where(sub == h, ct_all, 0.0), axis=0, keepdims=True)
    for i in range(seq // ATT_BLOCK):
        lo = i * ATT_BLOCK
        q = q_ref[lo:lo + ATT_BLOCK, :]
        out = _causal_attend(q, k_ref, v_ref, i, c_row[lo:lo + ATT_BLOCK, :], c_col)
        o_ref[lo:lo + ATT_BLOCK, :] = out.astype(o_ref.dtype)


def _fox_attention(qkv, c, ct, batch, seq):
    tokens = batch * seq
    kern = functools.partial(_fox_attn_kernel, seq=seq)
    return pl.pallas_call(
        kern,
        grid=(batch, FOX_HEADS),
        in_specs=[pl.BlockSpec((seq, HEAD_DIM), lambda b, h: (b, h)),
                  pl.BlockSpec((seq, HEAD_DIM), lambda b, h: (b, FOX_HEADS + h)),
                  pl.BlockSpec((seq, HEAD_DIM), lambda b, h: (b, 2 * FOX_HEADS + h)),
                  pl.BlockSpec((1, seq, LANES), lambda b, h: (b, 0, 0)),
                  pl.BlockSpec((1, FOX_HEADS, seq), lambda b, h: (b, 0, 0))],
        out_specs=pl.BlockSpec((seq, HEAD_DIM), lambda b, h: (b, h)),
        out_shape=jax.ShapeDtypeStruct((tokens, D_MODEL), jnp.bfloat16),
        compiler_params=_params("parallel", "parallel"),
        name="fox_attention",
    )(qkv, qkv, qkv, c, ct)


def _diff_attn_kernel(lam_ref, q1_ref, q2_ref, k1_ref, k2_ref, v_ref, cos_ref, sa_ref, sb_ref, g_ref,
                      o_ref, q1s, q2s, k1s, k2s, *, seq, out_scale):
    cos = cos_ref[...]
    sa = sa_ref[...]
    sb = sb_ref[...]
    half = ROT_DIM // 2
    for src, dst in ((q1_ref, q1s), (q2_ref, q2s), (k1_ref, k1s), (k2_ref, k2s)):
        x = src[...].astype(jnp.float32)
        rot = x * cos + pltpu.roll(x, half, axis=1) * sa + pltpu.roll(x, HEAD_DIM - half, axis=1) * sb
        dst[...] = rot.astype(jnp.bfloat16)
    lam = lam_ref[0]
    g = g_ref[...] * out_scale
    for i in range(seq // ATT_BLOCK):
        lo = i * ATT_BLOCK
        a1 = _causal_attend(q1s[lo:lo + ATT_BLOCK, :], k1s, v_ref, i, None, None)
        a2 = _causal_attend(q2s[lo:lo + ATT_BLOCK, :], k2s, v_ref, i, None, None)
        o = a1 - lam * a2
        o = o * lax.rsqrt(jnp.mean(jnp.square(o), axis=-1, keepdims=True) + SUBLN_EPS)
        o_ref[lo:lo + ATT_BLOCK, :] = (o * g).astype(o_ref.dtype)


def _diff_attention(qkv, lam, cos_t, sin_a, sin_b, subln_g, lam_init, batch, seq):
    tokens = batch * seq
    kern = functools.partial(_diff_attn_kernel, seq=seq, out_scale=1.0 - lam_init)
    kv_off = 2 * DIFF_HEADS
    v_off = 2 * DIFF_HEADS
    tab = pl.BlockSpec((seq, HEAD_DIM), lambda b, h: (0, 0))
    return pl.pallas_call(
        kern,
        grid=(batch, DIFF_HEADS),
        in_specs=[pl.BlockSpec(memory_space=pltpu.SMEM),
                  pl.BlockSpec((seq, HEAD_DIM), lambda b, h: (b, 2 * h)),
                  pl.BlockSpec((seq, HEAD_DIM), lambda b, h: (b, 2 * h + 1)),
                  pl.BlockSpec((seq, HEAD_DIM), lambda b, h: (b, kv_off + 2 * h)),
                  pl.BlockSpec((seq, HEAD_DIM), lambda b, h: (b, kv_off + 2 * h + 1)),
                  pl.BlockSpec((seq, 2 * HEAD_DIM), lambda b, h: (b, v_off + h)),
                  tab, tab, tab,
                  pl.BlockSpec((1, 2 * HEAD_DIM), lambda b, h: (0, 0))],
        out_specs=pl.BlockSpec((seq, 2 * HEAD_DIM), lambda b, h: (b, h)),
        out_shape=jax.ShapeDtypeStruct((tokens, D_MODEL), jnp.bfloat16),
        scratch_shapes=[pltpu.VMEM((seq, HEAD_DIM), jnp.bfloat16)] * 4,
        compiler_params=_params("parallel", "parallel"),
        name="diff_attention",
    )(lam, qkv, qkv, qkv, qkv, qkv, cos_t, sin_a, sin_b, subln_g)


def _deepnorm(z, g, b):
    mu = jnp.mean(z, axis=-1, keepdims=True)
    zc = z - mu
    var = jnp.mean(zc * zc, axis=-1, keepdims=True)
    return zc * lax.rsqrt(var + LN_EPS) * g + b


def _proj_ln_kernel(a_ref, w_ref, h_ref, g_ref, b_ref, of_ref, ob_ref, ol_ref, *, tm):
    y = jnp.dot(a_ref[...], w_ref[...], preferred_element_type=jnp.float32)
    out = _deepnorm(DEEPNORM_ALPHA * h_ref[...] + y, g_ref[...], b_ref[...])
    of_ref[...] = out
    ob_ref[...] = out.astype(jnp.bfloat16)
    for s in range(ROW_CHUNKS):
        ol_ref[pl.ds(s, tm, stride=ROW_CHUNKS), :] = out[:, s * LANES:(s + 1) * LANES]


def _proj_ln(a, w, h, g, b):
    tokens = a.shape[0]
    tm = 256
    row = pl.BlockSpec((tm, D_MODEL), lambda i: (i, 0))
    vec = pl.BlockSpec((1, D_MODEL), lambda i: (0, 0))
    return pl.pallas_call(
        functools.partial(_proj_ln_kernel, tm=tm),
        grid=(tokens // tm,),
        in_specs=[row, pl.BlockSpec((D_MODEL, D_MODEL), lambda i: (0, 0)), row, vec, vec],
        out_specs=[row, row, pl.BlockSpec((tm * ROW_CHUNKS, LANES), lambda i: (i, 0))],
        out_shape=[jax.ShapeDtypeStruct((tokens, D_MODEL), jnp.float32),
                   jax.ShapeDtypeStruct((tokens, D_MODEL), jnp.bfloat16),
                   jax.ShapeDtypeStruct((tokens * ROW_CHUNKS, LANES), jnp.float32)],
        compiler_params=_params("parallel"),
        name="outproj_deepnorm",
    )(a, w, h, g, b)


def _router_kernel(h_ref, wt_ref, bias_ref, idx_ref, w_ref, rank_ref, cnt_ref, carry_ref, *, tm):
    step = pl.program_id(0)

    @pl.when(step == 0)
    def _():
        carry_ref[...] = jnp.zeros_like(carry_ref)

    logits = lax.dot_general(wt_ref[...], h_ref[...], _NT, precision=_HIGHEST,
                             preferred_element_type=jnp.float32)
    scores = 1.0 / (1.0 + jnp.exp(-logits))
    biased = scores + bias_ref[...]
    npg = EXPERTS_PER_GROUP
    group = lax.broadcasted_iota(jnp.int32, (N_GROUPS, tm), 0)
    s = [scores[j * N_GROUPS:(j + 1) * N_GROUPS] for j in range(npg)]
    bz = [biased[j * N_GROUPS:(j + 1) * N_GROUPS] for j in range(npg)]
    eid = [group * npg + j for j in range(npg)]
    neg = -jnp.inf

    m1 = functools.reduce(jnp.maximum, bz)
    j1 = functools.reduce(jnp.minimum, [jnp.where(bz[j] == m1, j, npg) for j in range(npg)])
    m2 = functools.reduce(jnp.maximum, [jnp.where(j1 == j, neg, bz[j]) for j in range(npg)])
    gs = m1 + m2
    beaten = jnp.zeros((N_GROUPS, tm), jnp.int32)
    for r in range(1, N_GROUPS):
        og = pltpu.roll(gs, r, axis=0)
        oi = pltpu.roll(group, r, axis=0)
        wins = jnp.where(og > gs, 1, jnp.where(og == gs, jnp.where(oi < group, 1, 0), 0))
        beaten = beaten + wins
    keep = beaten < TOPK_GROUPS
    mb = [jnp.where(keep, bz[j], neg) for j in range(npg)]

    sel_idx, sel_w = [], []
    chosen = [jnp.zeros((N_GROUPS, tm), jnp.float32) for _ in range(npg)]
    for _ in range(TOP_K):
        m = jnp.max(functools.reduce(jnp.maximum, mb), axis=0, keepdims=True)
        cand = functools.reduce(jnp.minimum, [jnp.where(mb[j] == m, eid[j], N_EXPERTS) for j in range(npg)])
        ik = jnp.min(cand, axis=0, keepdims=True)
        hit = [eid[j] == ik for j in range(npg)]
        wk = functools.reduce(jnp.add, [jnp.where(hit[j], s[j], 0.0) for j in range(npg)])
        sel_idx.append(ik)
        sel_w.append(jnp.sum(wk, axis=0, keepdims=True))
        mb = [jnp.where(hit[j], neg, mb[j]) for j in range(npg)]
        chosen = [jnp.where(hit[j], 1.0, chosen[j]) for j in range(npg)]
    wsum = functools.reduce(jnp.add, sel_w)
    idx_ref[...] = jnp.concatenate(sel_idx, axis=0)
    w_ref[...] = jnp.concatenate(sel_w, axis=0) / wsum * ROUTED_SCALE

    onehot = jnp.concatenate(chosen, axis=0)
    tr = lax.broadcasted_iota(jnp.int32, (tm, tm), 0)
    tc = lax.broadcasted_iota(jnp.int32, (tm, tm), 1)
    before = (tr < tc).astype(jnp.bfloat16)
    excl = jnp.dot(onehot.astype(jnp.bfloat16), before, preferred_element_type=jnp.float32)
    base = excl + carry_ref[:, 0:1]
    bs = [base[j * N_GROUPS:(j + 1) * N_GROUPS] for j in range(npg)]
    ranks = []
    for k in range(TOP_K):
        rk = functools.reduce(jnp.add, [jnp.where(eid[j] == sel_idx[k], bs[j], 0.0) for j in range(npg)])
        ranks.append(jnp.sum(rk, axis=0, keepdims=True))
    rank_ref[...] = jnp.concatenate(ranks, axis=0).astype(jnp.int32)
    carry_ref[...] = carry_ref[...] + jnp.sum(onehot, axis=1, keepdims=True)
    cnt_ref[...] = carry_ref[...]


def _router(h, wt_perm, bias_perm):
    tokens = h.shape[0]
    tm = 512
    sel = pl.BlockSpec((TOP_K, tm), lambda i: (0, i))
    return pl.pallas_call(
        functools.partial(_router_kernel, tm=tm),
        grid=(tokens // tm,),
        in_specs=[pl.BlockSpec((tm, D_MODEL), lambda i: (i, 0)),
                  pl.BlockSpec((N_EXPERTS, D_MODEL), lambda i: (0, 0)),
                  pl.BlockSpec((N_EXPERTS, 1), lambda i: (0, 0))],
        out_specs=[sel, sel, sel, pl.BlockSpec((N_EXPERTS, LANES), lambda i: (0, 0))],
        out_shape=[jax.ShapeDtypeStruct((TOP_K, tokens), jnp.int32),
                   jax.ShapeDtypeStruct((TOP_K, tokens), jnp.float32),
                   jax.ShapeDtypeStruct((TOP_K, tokens), jnp.int32),
                   jax.ShapeDtypeStruct((N_EXPERTS, LANES), jnp.float32)],
        scratch_shapes=[pltpu.VMEM((N_EXPERTS, LANES), jnp.float32)],
        compiler_params=_params("arbitrary"),
        name="moe_router",
    )(h, wt_perm, bias_perm)


def _group_copy(src_ref, dst_ref, sem):
    return pltpu.make_async_copy(src_ref.at[pl.ds(0, COPY_GROUP)], dst_ref.at[pl.ds(0, COPY_GROUP)], sem)


def _dispatch_kernel(lo_ref, hi_ref, pos_ref, h3_ref, zero_ref, xs_ref, sem, zsem, *, tokens):
    step = pl.program_id(0)
    nstep = pl.num_programs(0)

    @pl.when(step > 0)
    def _():
        _group_copy(h3_ref, xs_ref, sem).wait()

    t0 = (step % (tokens // COPY_GROUP)) * COPY_GROUP

    def issue(r, carry):
        pltpu.make_async_copy(h3_ref.at[t0 + r], xs_ref.at[pos_ref[0, 0, r]], sem).start()
        return carry

    lax.fori_loop(0, COPY_GROUP, issue, 0)

    @pl.when(step == 0)
    def _():
        def per_expert(e, carry):
            def fill(r, c2):
                pltpu.make_async_copy(zero_ref, xs_ref.at[r], zsem).start()
                return c2

            def drain(r, c2):
                pltpu.make_async_copy(zero_ref, xs_ref.at[0], zsem).wait()
                return c2

            lax.fori_loop(lo_ref[e], hi_ref[e], fill, 0)
            lax.fori_loop(lo_ref[e], hi_ref[e], drain, 0)
            return carry

        lax.fori_loop(0, N_EXPERTS, per_expert, 0)

    @pl.when(step == nstep - 1)
    def _():
        _group_copy(h3_ref, xs_ref, sem).wait()


def _dispatch(pad_lo, pad_hi, pos, h_lin, n_rows):
    tokens = h_lin.shape[0] // ROW_CHUNKS
    n_assign = pos.size
    h3 = h_lin.reshape(tokens, ROW_CHUNKS, LANES)
    zero = jnp.zeros((ROW_CHUNKS, LANES), jnp.float32)
    xs = pl.pallas_call(
        functools.partial(_dispatch_kernel, tokens=tokens),
        grid_spec=pltpu.PrefetchScalarGridSpec(
            num_scalar_prefetch=2,
            grid=(n_assign // COPY_GROUP,),
            in_specs=[pl.BlockSpec((1, 1, COPY_GROUP), lambda i, lo, hi: (i, 0, 0), memory_space=pltpu.SMEM),
                      pl.BlockSpec(memory_space=pl.ANY),
                      pl.BlockSpec(memory_space=pl.ANY)],
            out_specs=pl.BlockSpec(memory_space=pl.ANY),
            scratch_shapes=[pltpu.SemaphoreType.DMA, pltpu.SemaphoreType.DMA]),
        out_shape=jax.ShapeDtypeStruct((n_rows, ROW_CHUNKS, LANES), jnp.float32),
        compiler_params=_params("arbitrary"),
        name="moe_dispatch",
    )(pad_lo, pad_hi, pos.reshape(n_assign // COPY_GROUP, 1, COPY_GROUP), h3, zero)
    return xs.reshape(n_rows * ROW_CHUNKS, LANES)


def _return_kernel(pos_ref, ys_ref, out_ref, sem):
    step = pl.program_id(0)
    nstep = pl.num_programs(0)

    @pl.when(step > 0)
    def _():
        _group_copy(ys_ref, out_ref, sem).wait()

    i0 = step * COPY_GROUP

    def issue(r, carry):
        pltpu.make_async_copy(ys_ref.at[pos_ref[0, 0, r]], out_ref.at[i0 + r], sem).start()
        return carry

    lax.fori_loop(0, COPY_GROUP, issue, 0)

    @pl.when(step == nstep - 1)
    def _():
        _group_copy(ys_ref, out_ref, sem).wait()


def _return_rows(pos, y_lin):
    n_rows = y_lin.shape[0] // ROW_CHUNKS
    n_assign = pos.size
    out = pl.pallas_call(
        _return_kernel,
        grid=(n_assign // COPY_GROUP,),
        in_specs=[pl.BlockSpec((1, 1, COPY_GROUP), lambda i: (i, 0, 0), memory_space=pltpu.SMEM),
                  pl.BlockSpec(memory_space=pl.ANY)],
        out_specs=pl.BlockSpec(memory_space=pl.ANY),
        out_shape=jax.ShapeDtypeStruct((n_assign, ROW_CHUNKS, LANES), jnp.float32),
        scratch_shapes=[pltpu.SemaphoreType.DMA],
        compiler_params=_params("arbitrary"),
        name="moe_return",
    )(pos.reshape(n_assign // COPY_GROUP, 1, COPY_GROUP), y_lin.reshape(n_rows, ROW_CHUNKS, LANES))
    return out


def _silu_mul(gate_up):
    half = gate_up.shape[-1] // 2
    gate = gate_up[:, :half]
    up = gate_up[:, half:]
    return (gate / (1.0 + jnp.exp(-gate)) * up).astype(jnp.bfloat16)


def _expert_kernel(be_ref, nu_ref, x_ref, wgu_ref, wd_ref, y_ref):
    blk = EXPERT_BLOCK
    b = pl.program_id(0)

    @pl.when(b < nu_ref[0])
    def _():
        x = jnp.concatenate([x_ref[pl.ds(s, blk, stride=ROW_CHUNKS), :] for s in range(ROW_CHUNKS)],
                            axis=-1).astype(jnp.bfloat16)
        act = _silu_mul(jnp.dot(x, wgu_ref[0], preferred_element_type=jnp.float32))
        y = jnp.dot(act, wd_ref[0], preferred_element_type=jnp.float32)
        for s in range(ROW_CHUNKS):
            y_ref[pl.ds(s, blk, stride=ROW_CHUNKS), :] = y[:, s * LANES:(s + 1) * LANES]

    @pl.when(b >= nu_ref[0])
    def _():
        y_ref[...] = jnp.zeros_like(y_ref)


def _experts(block_expert, n_used, x_lin, wgu, wd):
    n_blocks = block_expert.shape[0]
    rows = EXPERT_BLOCK * ROW_CHUNKS
    return pl.pallas_call(
        _expert_kernel,
        grid_spec=pltpu.PrefetchScalarGridSpec(
            num_scalar_prefetch=2,
            grid=(n_blocks,),
            in_specs=[pl.BlockSpec((rows, LANES), lambda b, be, nu: (b, 0)),
                      pl.BlockSpec((1, D_MODEL, 2 * EXPERT_DIM), lambda b, be, nu: (be[b], 0, 0)),
                      pl.BlockSpec((1, EXPERT_DIM, D_MODEL), lambda b, be, nu: (be[b], 0, 0))],
            out_specs=pl.BlockSpec((rows, LANES), lambda b, be, nu: (b, 0))),
        out_shape=jax.ShapeDtypeStruct(x_lin.shape, jnp.float32),
        compiler_params=_params("arbitrary"),
        name="moe_experts",
    )(block_expert, n_used, x_lin, wgu, wd)


def _moe_out_kernel(hb_ref, h_ref, yk_ref, w_ref, wsgu_ref, wsd_ref, g_ref, b_ref, of_ref, ob_ref, *, tm):
    act = _silu_mul(jnp.dot(hb_ref[...], wsgu_ref[...], preferred_element_type=jnp.float32))
    y = jnp.dot(act, wsd_ref[...], preferred_element_type=jnp.float32)
    w = w_ref[...]
    cols = []
    for s in range(ROW_CHUNKS):
        acc = jnp.zeros((tm, LANES), jnp.float32)
        for k in range(TOP_K):
            acc = acc + w[:, k:k + 1] * yk_ref[k, pl.ds(s, tm, stride=ROW_CHUNKS), :]
        cols.append(acc)
    y = y + jnp.concatenate(cols, axis=-1)
    out = _deepnorm(DEEPNORM_ALPHA * h_ref[...] + y, g_ref[...], b_ref[...])
    of_ref[...] = out
    ob_ref[...] = out.astype(jnp.bfloat16)


def _moe_out(hb, h, yk, w_tk, wsgu, wsd, g, b):
    tokens = h.shape[0]
    tm = 128
    row = pl.BlockSpec((tm, D_MODEL), lambda i: (i, 0))
    vec = pl.BlockSpec((1, D_MODEL), lambda i: (0, 0))
    return pl.pallas_call(
        functools.partial(_moe_out_kernel, tm=tm),
        grid=(tokens // tm,),
        in_specs=[row, row,
                  pl.BlockSpec((TOP_K, tm * ROW_CHUNKS, LANES), lambda i: (0, i, 0)),
                  pl.BlockSpec((tm, TOP_K), lambda i: (i, 0)),
                  pl.BlockSpec((D_MODEL, 2 * EXPERT_DIM), lambda i: (0, 0)),
                  pl.BlockSpec((EXPERT_DIM, D_MODEL), lambda i: (0, 0)),
                  vec, vec],
        out_specs=[row, row],
        out_shape=[jax.ShapeDtypeStruct((tokens, D_MODEL), jnp.float32),
                   jax.ShapeDtypeStruct((tokens, D_MODEL), jnp.bfloat16)],
        compiler_params=_params("parallel"),
        name="moe_combine_deepnorm",
    )(hb, h, yk, w_tk, wsgu, wsd, g, b)


def _moe_layer(h, hb, h_lin, w_router, b_router, w_gate_up, w_down, ws_gate_up, ws_down, g, b):
    tokens = h.shape[0]
    n_assign = tokens * TOP_K
    n_blocks = n_assign // EXPERT_BLOCK + N_EXPERTS
    n_rows = n_blocks * EXPERT_BLOCK
    perm = (jnp.arange(N_EXPERTS) % N_GROUPS) * EXPERTS_PER_GROUP + jnp.arange(N_EXPERTS) // N_GROUPS
    top_idx, top_w, rank, cnt_perm = _router(h, w_router.T[perm], b_router[perm].reshape(N_EXPERTS, 1))

    counts = jnp.zeros((N_EXPERTS,), jnp.int32).at[perm].set(cnt_perm[:, 0].astype(jnp.int32))
    padded = (counts + EXPERT_BLOCK - 1) // EXPERT_BLOCK * EXPERT_BLOCK
    padded_end = jnp.cumsum(padded)
    padded_start = padded_end - padded
    pos = (jnp.take(padded_start, top_idx) + rank).reshape(n_assign)
    block_expert = jnp.minimum(
        jnp.searchsorted(padded_end, jnp.arange(n_blocks, dtype=jnp.int32) * EXPERT_BLOCK, side="right"),
        N_EXPERTS - 1).astype(jnp.int32)
    n_used = (padded_end[-1:] // EXPERT_BLOCK).astype(jnp.int32)

    x_lin = _dispatch((padded_start + counts).astype(jnp.int32), padded_end.astype(jnp.int32), pos, h_lin, n_rows)
    y_lin = _experts(block_expert, n_used, x_lin, w_gate_up.astype(jnp.bfloat16), w_down.astype(jnp.bfloat16))
    yk = _return_rows(pos, y_lin).reshape(TOP_K, tokens * ROW_CHUNKS, LANES)
    return _moe_out(hb, h, yk, top_w.T, ws_gate_up.astype(jnp.bfloat16), ws_down.astype(jnp.bfloat16),
                    g.reshape(1, D_MODEL), b.reshape(1, D_MODEL))


def _q_scale(width, q_cols):
    return jnp.where(jnp.arange(width) < q_cols, HEAD_DIM ** -0.5, 1.0).astype(jnp.float32).reshape(1, width)


def _fox_mixer(hb, w_in, b_forget, batch, seq):
    width = 3 * D_MODEL
    qkv = _matmul(hb, w_in[:, :width].astype(jnp.bfloat16), _q_scale(width, D_MODEL), jnp.bfloat16,
                  1024, 512, "fox_in_proj")
    w_f = jnp.pad(w_in[:, width:], ((0, 0), (0, LANES - FOX_HEADS))).astype(jnp.bfloat16)
    f_logit = _matmul(hb, w_f, jnp.ones((1, LANES), jnp.float32), jnp.float32, 1024, LANES, "fox_gate_proj")
    b_pad = jnp.pad(b_forget, (0, LANES - FOX_HEADS)).reshape(1, LANES)
    c, ct = _forget_cumsum(f_logit, b_pad, batch, seq)
    return _fox_attention(qkv, c, ct, batch, seq)


def _rotary_tables(seq):
    half = ROT_DIM // 2
    inv_freq = ROPE_THETA ** (-jnp.arange(0, ROT_DIM, 2, dtype=jnp.float32) / ROT_DIM)
    ang = jnp.arange(seq, dtype=jnp.float32)[:, None] * inv_freq[None, :]
    cos, sin = jnp.cos(ang), jnp.sin(ang)
    zeros = jnp.zeros((seq, HEAD_DIM - ROT_DIM), jnp.float32)
    zh = jnp.zeros((seq, half), jnp.float32)
    cos_t = jnp.concatenate([cos, cos, jnp.ones_like(zeros)], axis=1)
    sin_a = jnp.concatenate([zh, sin, zeros], axis=1)
    sin_b = jnp.concatenate([-sin, zh, zeros], axis=1)
    return cos_t, sin_a, sin_b


def _diff_mixer(hb, layer_idx, w_in, lambda_qk, subln_g, tables, batch, seq):
    width = 3 * D_MODEL
    qkv = _matmul(hb, w_in.astype(jnp.bfloat16), _q_scale(width, D_MODEL), jnp.bfloat16, 1024, 512,
                  "diff_in_proj")
    lam_init = 0.8 - 0.6 * math.exp(-0.3 * layer_idx)
    lq = lambda_qk.astype(jnp.float32)
    lam = (jnp.exp(jnp.sum(lq[0] * lq[1])) - jnp.exp(jnp.sum(lq[2] * lq[3])) + lam_init).reshape(1)
    return _diff_attention(qkv, lam, *tables, subln_g.reshape(1, 2 * HEAD_DIM), lam_init, batch, seq)


def kernel(x, fox_w_in_0, fox_b_forget_0, fox_w_out_0, ln_mix_g_0, ln_mix_b_0, moe_w_router_0, moe_b_router_0, moe_w_gate_up_0, moe_w_down_0, moe_ws_gate_up_0, moe_ws_down_0, ln_ffn_g_0, ln_ffn_b_0, diff_w_in_1, diff_lambda_qk_1, diff_subln_g_1, diff_w_out_1, ln_mix_g_1, ln_mix_b_1, moe_w_router_1, moe_b_router_1, moe_w_gate_up_1, moe_w_down_1, moe_ws_gate_up_1, moe_ws_down_1, ln_ffn_g_1, ln_ffn_b_1, fox_w_in_2, fox_b_forget_2, fox_w_out_2, ln_mix_g_2, ln_mix_b_2, moe_w_router_2, moe_b_router_2, moe_w_gate_up_2, moe_w_down_2, moe_ws_gate_up_2, moe_ws_down_2, ln_ffn_g_2, ln_ffn_b_2, diff_w_in_3, diff_lambda_qk_3, diff_subln_g_3, diff_w_out_3, ln_mix_g_3, ln_mix_b_3, moe_w_router_3, moe_b_router_3, moe_w_gate_up_3, moe_w_down_3, moe_ws_gate_up_3, moe_ws_down_3, ln_ffn_g_3, ln_ffn_b_3):
    batch, seq, _ = x.shape
    tokens = batch * seq
    mix = [(fox_w_in_0, fox_b_forget_0, fox_w_out_0),
           (diff_w_in_1, diff_lambda_qk_1, diff_subln_g_1, diff_w_out_1),
           (fox_w_in_2, fox_b_forget_2, fox_w_out_2),
           (diff_w_in_3, diff_lambda_qk_3, diff_subln_g_3, diff_w_out_3)]
    norm_mix = [(ln_mix_g_0, ln_mix_b_0), (ln_mix_g_1, ln_mix_b_1), (ln_mix_g_2, ln_mix_b_2), (ln_mix_g_3, ln_mix_b_3)]
    moe = [(moe_w_router_0, moe_b_router_0, moe_w_gate_up_0, moe_w_down_0, moe_ws_gate_up_0, moe_ws_down_0),
           (moe_w_router_1, moe_b_router_1, moe_w_gate_up_1, moe_w_down_1, moe_ws_gate_up_1, moe_ws_down_1),
           (moe_w_router_2, moe_b_router_2, moe_w_gate_up_2, moe_w_down_2, moe_ws_gate_up_2, moe_ws_down_2),
           (moe_w_router_3, moe_b_router_3, moe_w_gate_up_3, moe_w_down_3, moe_ws_gate_up_3, moe_ws_down_3)]
    norm_ffn = [(ln_ffn_g_0, ln_ffn_b_0), (ln_ffn_g_1, ln_ffn_b_1), (ln_ffn_g_2, ln_ffn_b_2), (ln_ffn_g_3, ln_ffn_b_3)]

    tables = _rotary_tables(seq)
    h = x.reshape(tokens, D_MODEL)
    hb = h.astype(jnp.bfloat16)
    for i in range(DEPTH):
        if i % 2 == 0:
            w_in, b_forget, w_out = mix[i]
            o = _fox_mixer(hb, w_in, b_forget, batch, seq)
        else:
            w_in, lambda_qk, subln_g, w_out = mix[i]
            o = _diff_mixer(hb, i, w_in, lambda_qk, subln_g, tables, batch, seq)
        g, b = norm_mix[i]
        h, hb, h_lin = _proj_ln(o, w_out.astype(jnp.bfloat16), h, g.reshape(1, D_MODEL), b.reshape(1, D_MODEL))
        h, hb = _moe_layer(h, hb, h_lin, *moe[i], *norm_ffn[i])
    return h.reshape(batch, seq, D_MODEL)
```

```python
import functools
import math

import jax
import jax.numpy as jnp
from jax import lax
from jax.experimental import pallas as pl
from jax.experimental.pallas import tpu as pltpu

D_MODEL = 2048
DEPTH = 4
HEAD_DIM = 128
LANES = 128
ROW_CHUNKS = D_MODEL // LANES
FOX_HEADS = D_MODEL // HEAD_DIM
DIFF_HEADS = D_MODEL // (2 * HEAD_DIM)
ROT_DIM = HEAD_DIM // 4
ROPE_THETA = 500000.0
N_EXPERTS = 64
TOP_K = 8
N_GROUPS = 8
TOPK_GROUPS = 4
EXPERTS_PER_GROUP = N_EXPERTS // N_GROUPS
EXPERT_DIM = D_MODEL // 4
ROUTED_SCALE = 2.5
DEEPNORM_ALPHA = (2 * DEPTH) ** 0.25
LN_EPS = 1e-5
SUBLN_EPS = 1e-5

ATT_BLOCK = 256
EXPERT_BLOCK = 256
VMEM_LIMIT = 56 * 1024 * 1024

_HIGHEST = lax.Precision.HIGHEST
_NT = (((1,), (1,)), ((), ()))


def _params(*sem):
    return pltpu.CompilerParams(dimension_semantics=sem, vmem_limit_bytes=VMEM_LIMIT)


def _mm_kernel(a_ref, b_ref, s_ref, o_ref):
    acc = jnp.dot(a_ref[...], b_ref[...], preferred_element_type=jnp.float32)
    o_ref[...] = (acc * s_ref[...]).astype(o_ref.dtype)


def _matmul(a, b, col_scale, out_dtype, tm, tn, name):
    m, k = a.shape
    n = b.shape[1]
    return pl.pallas_call(
        _mm_kernel,
        grid=(m // tm, n // tn),
        in_specs=[pl.BlockSpec((tm, k), lambda i, j: (i, 0)),
                  pl.BlockSpec((k, tn), lambda i, j: (0, j)),
                  pl.BlockSpec((1, tn), lambda i, j: (0, j))],
        out_specs=pl.BlockSpec((tm, tn), lambda i, j: (i, j)),
        out_shape=jax.ShapeDtypeStruct((m, n), out_dtype),
        compiler_params=_params("parallel", "parallel"),
        name=name,
    )(a, b, col_scale)


def _fgate_kernel(fl_ref, b_ref, c_ref, ct_ref, *, seq, chunk):
    x = fl_ref[...] + b_ref[...]
    ls = -(jnp.maximum(-x, 0.0) + jnp.log1p(jnp.exp(-jnp.abs(x))))
    r = lax.broadcasted_iota(jnp.int32, (chunk, chunk), 0)
    c = lax.broadcasted_iota(jnp.int32, (chunk, chunk), 1)
    tri = (r >= c).astype(jnp.float32)
    carry = jnp.zeros((1, LANES), jnp.float32)
    for ch in range(seq // chunk):
        blk = jnp.dot(tri, ls[ch * chunk:(ch + 1) * chunk], precision=_HIGHEST,
                      preferred_element_type=jnp.float32) + carry
        carry = blk[chunk - 1:chunk, :]
        c_ref[0, ch * chunk:(ch + 1) * chunk, :] = blk
        ct_ref[0, :, ch * chunk:(ch + 1) * chunk] = blk.T[:FOX_HEADS]


def _forget_cumsum(f_logit, b_pad, batch, seq):
    chunk = 256
    return pl.pallas_call(
        functools.partial(_fgate_kernel, seq=seq, chunk=chunk),
        grid=(batch,),
        in_specs=[pl.BlockSpec((seq, LANES), lambda b: (b, 0)),
                  pl.BlockSpec((1, LANES), lambda b: (0, 0))],
        out_specs=[pl.BlockSpec((1, seq, LANES), lambda b: (b, 0, 0)),
                   pl.BlockSpec((1, FOX_HEADS, seq), lambda b: (b, 0, 0))],
        out_shape=[jax.ShapeDtypeStruct((batch, seq, LANES), jnp.float32),
                   jax.ShapeDtypeStruct((batch, FOX_HEADS, seq), jnp.float32)],
        compiler_params=_params("parallel"),
        name="forget_cumsum",
    )(f_logit, b_pad)


def _causal_attend(q, k_ref, v_ref, i, row_bias, col_bias):
    blk = ATT_BLOCK
    lo = i * blk
    s_d = lax.dot_general(q, k_ref[lo:lo + blk, :], _NT, preferred_element_type=jnp.float32)
    if row_bias is not None:
        s_d = s_d + row_bias - col_bias[:, lo:lo + blk]
    r = lax.broadcasted_iota(jnp.int32, (blk, blk), 0)
    c = lax.broadcasted_iota(jnp.int32, (blk, blk), 1)
    s_d = jnp.where(c <= r, s_d, -jnp.inf)
    m = jnp.max(s_d, axis=-1, keepdims=True)
    if i > 0:
        s_o = lax.dot_general(q, k_ref[0:lo, :], _NT, preferred_element_type=jnp.float32)
        if row_bias is not None:
            s_o = s_o + row_bias - col_bias[:, 0:lo]
        m = jnp.maximum(m, jnp.max(s_o, axis=-1, keepdims=True))
    p_d = jnp.exp(s_d - m)
    l = jnp.sum(p_d, axis=-1, keepdims=True)
    acc = jnp.dot(p_d.astype(jnp.bfloat16), v_ref[lo:lo + blk, :], preferred_element_type=jnp.float32)
    if i > 0:
        p_o = jnp.exp(s_o - m)
        l = l + jnp.sum(p_o, axis=-1, keepdims=True)
        acc = acc + jnp.dot(p_o.astype(jnp.bfloat16), v_ref[0:lo, :], preferred_element_type=jnp.float32)
    return acc / l


def _fox_attn_kernel(q_ref, k_ref, v_ref, c_ref, ct_ref, o_ref, *, seq):
    h = pl.program_id(1)
    c_all = c_ref[0]
    lane = lax.broadcasted_iota(jnp.int32, c_all.shape, 1)
    c_row = jnp.sum(jnp.where(lane == h, c_all, 0.0), axis=1, keepdims=True)
    ct_all = ct_ref[0]
    sub = lax.broadcasted_iota(jnp.int32, ct_all.shape, 0)
    c_col = jnp.sum(jnp.where(sub == h, ct_all, 0.0), axis=0, keepdims=True)
    for i in range(seq // ATT_BLOCK):
        lo = i * ATT_BLOCK
        q = q_ref[lo:lo + ATT_BLOCK, :]
        out = _causal_attend(q, k_ref, v_ref, i, c_row[lo:lo + ATT_BLOCK, :], c_col)
        o_ref[lo:lo + ATT_BLOCK, :] = out.astype(o_ref.dtype)


def _fox_attention(qkv, c, ct, batch, seq):
    tokens = batch * seq
    kern = functools.partial(_fox_attn_kernel, seq=seq)
    return pl.pallas_call(
        kern,
        grid=(batch, FOX_HEADS),
        in_specs=[pl.BlockSpec((seq, HEAD_DIM), lambda b, h: (b, h)),
                  pl.BlockSpec((seq, HEAD_DIM), lambda b, h: (b, FOX_HEADS + h)),
                  pl.BlockSpec((seq, HEAD_DIM), lambda b, h: (b, 2 * FOX_HEADS + h)),
                  pl.BlockSpec((1, seq, LANES), lambda b, h: (b, 0, 0)),
                  pl.BlockSpec((1, FOX_HEADS, seq), lambda b, h: (b, 0, 0))],
        out_specs=pl.BlockSpec((seq, HEAD_DIM), lambda b, h: (b, h)),
        out_shape=jax.ShapeDtypeStruct((tokens, D_MODEL), jnp.bfloat16),
        compiler_params=_params("parallel", "parallel"),
        name="fox_attention",
    )(qkv, qkv, qkv, c, ct)


def _diff_attn_kernel(lam_ref, q1_ref, q2_ref, k1_ref, k2_ref, v_ref, cos_ref, sa_ref, sb_ref, g_ref,
                      o_ref, q1s, q2s, k1s, k2s, *, seq, out_scale):
    cos = cos_ref[...]
    sa = sa_ref[...]
    sb = sb_ref[...]
    half = ROT_DIM // 2
    for src, dst in ((q1_ref, q1s), (q2_ref, q2s), (k1_ref, k1s), (k2_ref, k2s)):
        x = src[...].astype(jnp.float32)
        rot = x * cos + pltpu.roll(x, half, axis=1) * sa + pltpu.roll(x, HEAD_DIM - half, axis=1) * sb
        dst[...] = rot.astype(jnp.bfloat16)
    lam = lam_ref[0]
    g = g_ref[...] * out_scale
    for i in range(seq // ATT_BLOCK):
        lo = i * ATT_BLOCK
        a1 = _causal_attend(q1s[lo:lo + ATT_BLOCK, :], k1s, v_ref, i, None, None)
        a2 = _causal_attend(q2s[lo:lo + ATT_BLOCK, :], k2s, v_ref, i, None, None)
        o = a1 - lam * a2
        o = o * lax.rsqrt(jnp.mean(jnp.square(o), axis=-1, keepdims=True) + SUBLN_EPS)
        o_ref[lo:lo + ATT_BLOCK, :] = (o * g).astype(o_ref.dtype)


def _diff_attention(qkv, lam, cos_t, sin_a, sin_b, subln_g, lam_init, batch, seq):
    tokens = batch * seq
    kern = functools.partial(_diff_attn_kernel, seq=seq, out_scale=1.0 - lam_init)
    kv_off = 2 * DIFF_HEADS
    v_off = 2 * DIFF_HEADS
    tab = pl.BlockSpec((seq, HEAD_DIM), lambda b, h: (0, 0))
    return pl.pallas_call(
        kern,
        grid=(batch, DIFF_HEADS),
        in_specs=[pl.BlockSpec(memory_space=pltpu.SMEM),
                  pl.BlockSpec((seq, HEAD_DIM), lambda b, h: (b, 2 * h)),
                  pl.BlockSpec((seq, HEAD_DIM), lambda b, h: (b, 2 * h + 1)),
                  pl.BlockSpec((seq, HEAD_DIM), lambda b, h: (b, kv_off + 2 * h)),
                  pl.BlockSpec((seq, HEAD_DIM), lambda b, h: (b, kv_off + 2 * h + 1)),
                  pl.BlockSpec((seq, 2 * HEAD_DIM), lambda b, h: (b, v_off + h)),
                  tab, tab, tab,
                  pl.BlockSpec((1, 2 * HEAD_DIM), lambda b, h: (0, 0))],
        out_specs=pl.BlockSpec((seq, 2 * HEAD_DIM), lambda b, h: (b, h)),
        out_shape=jax.ShapeDtypeStruct((tokens, D_MODEL), jnp.bfloat16),
        scratch_shapes=[pltpu.VMEM((seq, HEAD_DIM), jnp.bfloat16)] * 4,
        compiler_params=_params("parallel", "parallel"),
        name="diff_attention",
    )(lam, qkv, qkv, qkv, qkv, qkv, cos_t, sin_a, sin_b, subln_g)


def _deepnorm(z, g, b):
    mu = jnp.mean(z, axis=-1, keepdims=True)
    zc = z - mu
    var = jnp.mean(zc * zc, axis=-1, keepdims=True)
    return zc * lax.rsqrt(var + LN_EPS) * g + b


def _proj_ln_kernel(a_ref, w_ref, h_ref, g_ref, b_ref, of_ref, ob_ref, ol_ref, *, tm):
    y = jnp.dot(a_ref[...], w_ref[...], preferred_element_type=jnp.float32)
    out = _deepnorm(DEEPNORM_ALPHA * h_ref[...] + y, g_ref[...], b_ref[...])
    of_ref[...] = out
    ob_ref[...] = out.astype(jnp.bfloat16)
    for s in range(ROW_CHUNKS):
        ol_ref[pl.ds(s, tm, stride=ROW_CHUNKS), :] = out[:, s * LANES:(s + 1) * LANES]


def _proj_ln(a, w, h, g, b):
    tokens = a.shape[0]
    tm = 256
    row = pl.BlockSpec((tm, D_MODEL), lambda i: (i, 0))
    vec = pl.BlockSpec((1, D_MODEL), lambda i: (0, 0))
    return pl.pallas_call(
        functools.partial(_proj_ln_kernel, tm=tm),
        grid=(tokens // tm,),
        in_specs=[row, pl.BlockSpec((D_MODEL, D_MODEL), lambda i: (0, 0)), row, vec, vec],
        out_specs=[row, row, pl.BlockSpec((tm * ROW_CHUNKS, LANES), lambda i: (i, 0))],
        out_shape=[jax.ShapeDtypeStruct((tokens, D_MODEL), jnp.float32),
                   jax.ShapeDtypeStruct((tokens, D_MODEL), jnp.bfloat16),
                   jax.ShapeDtypeStruct((tokens * ROW_CHUNKS, LANES), jnp.float32)],
        compiler_params=_params("parallel"),
        name="outproj_deepnorm",
    )(a, w, h, g, b)


def _router_kernel(h_ref, wt_ref, bias_ref, idx_ref, w_ref, rank_ref, cnt_ref, carry_ref, *, tm):
    step = pl.program_id(0)

    @pl.when(step == 0)
    def _():
        carry_ref[...] = jnp.zeros_like(carry_ref)

    logits = lax.dot_general(wt_ref[...], h_ref[...], _NT, precision=_HIGHEST,
                             preferred_element_type=jnp.float32)
    scores = 1.0 / (1.0 + jnp.exp(-logits))
    biased = scores + bias_ref[...]
    npg = EXPERTS_PER_GROUP
    group = lax.broadcasted_iota(jnp.int32, (N_GROUPS, tm), 0)
    s = [scores[j * N_GROUPS:(j + 1) * N_GROUPS] for j in range(npg)]
    bz = [biased[j * N_GROUPS:(j + 1) * N_GROUPS] for j in range(npg)]
    eid = [group * npg + j for j in range(npg)]
    neg = -jnp.inf

    m1 = functools.reduce(jnp.maximum, bz)
    j1 = functools.reduce(jnp.minimum, [jnp.where(bz[j] == m1, j, npg) for j in range(npg)])
    m2 = functools.reduce(jnp.maximum, [jnp.where(j1 == j, neg, bz[j]) for j in range(npg)])
    gs = m1 + m2
    beaten = jnp.zeros((N_GROUPS, tm), jnp.int32)
    for r in range(1, N_GROUPS):
        og = pltpu.roll(gs, r, axis=0)
        oi = pltpu.roll(group, r, axis=0)
        wins = jnp.where(og > gs, 1, jnp.where(og == gs, jnp.where(oi < group, 1, 0), 0))
        beaten = beaten + wins
    keep = beaten < TOPK_GROUPS
    mb = [jnp.where(keep, bz[j], neg) for j in range(npg)]

    sel_idx, sel_w = [], []
    chosen = [jnp.zeros((N_GROUPS, tm), jnp.float32) for _ in range(npg)]
    for _ in range(TOP_K):
        m = jnp.max(functools.reduce(jnp.maximum, mb), axis=0, keepdims=True)
        cand = functools.reduce(jnp.minimum, [jnp.where(mb[j] == m, eid[j], N_EXPERTS) for j in range(npg)])
        ik = jnp.min(cand, axis=0, keepdims=True)
        hit = [eid[j] == ik for j in range(npg)]
        wk = functools.reduce(jnp.add, [jnp.where(hit[j], s[j], 0.0) for j in range(npg)])
        sel_idx.append(ik)
        sel_w.append(jnp.sum(wk, axis=0, keepdims=True))
        mb = [jnp.where(hit[j], neg, mb[j]) for j in range(npg)]
        chosen = [jnp.where(hit[j], 1.0, chosen[j]) for j in range(npg)]
    wsum = functools.reduce(jnp.add, sel_w)
    idx_ref[...] = jnp.concatenate(sel_idx, axis=0)
    w_ref[...] = jnp.concatenate(sel_w, axis=0) / wsum * ROUTED_SCALE

    onehot = jnp.concatenate(chosen, axis=0)
    tr = lax.broadcasted_iota(jnp.int32, (tm, tm), 0)
    tc = lax.broadcasted_iota(jnp.int32, (tm, tm), 1)
    before = (tr < tc).astype(jnp.bfloat16)
    excl = jnp.dot(onehot.astype(jnp.bfloat16), before, preferred_element_type=jnp.float32)
    base = excl + carry_ref[:, 0:1]
    bs = [base[j * N_GROUPS:(j + 1) * N_GROUPS] for j in range(npg)]
    ranks = []
    for k in range(TOP_K):
        rk = functools.reduce(jnp.add, [jnp.where(eid[j] == sel_idx[k], bs[j], 0.0) for j in range(npg)])
        ranks.append(jnp.sum(rk, axis=0, keepdims=True))
    rank_ref[...] = jnp.concatenate(ranks, axis=0).astype(jnp.int32)
    carry_ref[...] = carry_ref[...] + jnp.sum(onehot, axis=1, keepdims=True)
    cnt_ref[...] = carry_ref[...]


def _router(h, wt_perm, bias_perm):
    tokens = h.shape[0]
    tm = 512
    sel = pl.BlockSpec((TOP_K, tm), lambda i: (0, i))
    return pl.pallas_call(
        functools.partial(_router_kernel, tm=tm),
        grid=(tokens // tm,),
        in_specs=[pl.BlockSpec((tm, D_MODEL), lambda i: (i, 0)),
                  pl.BlockSpec((N_EXPERTS, D_MODEL), lambda i: (0, 0)),
                  pl.BlockSpec((N_EXPERTS, 1), lambda i: (0, 0))],
        out_specs=[sel, sel, sel, pl.BlockSpec((N_EXPERTS, LANES), lambda i: (0, 0))],
        out_shape=[jax.ShapeDtypeStruct((TOP_K, tokens), jnp.int32),
                   jax.ShapeDtypeStruct((TOP_K, tokens), jnp.float32),
                   jax.ShapeDtypeStruct((TOP_K, tokens), jnp.int32),
                   jax.ShapeDtypeStruct((N_EXPERTS, LANES), jnp.float32)],
        scratch_shapes=[pltpu.VMEM((N_EXPERTS, LANES), jnp.float32)],
        compiler_params=_params("arbitrary"),
        name="moe_router",
    )(h, wt_perm, bias_perm)


def _dispatch_kernel(lo_ref, hi_ref, pos_ref, h3_ref, xs_ref, zbuf, sem, zsem, *, tt):
    step = pl.program_id(0)

    def issue(t, carry):
        for k in range(TOP_K):
            pltpu.make_async_copy(h3_ref.at[t], xs_ref.at[pos_ref[k, t]], sem).start()
        return carry

    lax.fori_loop(0, tt, issue, 0)

    @pl.when(step == 0)
    def _():
        zbuf[...] = jnp.zeros_like(zbuf)

        def per_range(e, carry):
            def fill(r, c2):
                pltpu.make_async_copy(zbuf, xs_ref.at[r], zsem).start()
                return c2

            def drain(r, c2):
                pltpu.make_async_copy(zbuf, xs_ref.at[0], zsem).wait()
                return c2

            lax.fori_loop(lo_ref[e], hi_ref[e], fill, 0)
            lax.fori_loop(lo_ref[e], hi_ref[e], drain, 0)
            return carry

        lax.fori_loop(0, N_EXPERTS + 1, per_range, 0)

    rows = pl.ds(0, tt * TOP_K)
    pltpu.make_async_copy(xs_ref.at[rows], xs_ref.at[rows], sem).wait()


def _dispatch(pad_lo, pad_hi, pos, h_lin, n_rows):
    tokens = h_lin.shape[0] // ROW_CHUNKS
    tt = 256
    h3 = h_lin.reshape(tokens, ROW_CHUNKS, LANES)
    xs = pl.pallas_call(
        functools.partial(_dispatch_kernel, tt=tt),
        grid_spec=pltpu.PrefetchScalarGridSpec(
            num_scalar_prefetch=2,
            grid=(tokens // tt,),
            in_specs=[pl.BlockSpec((TOP_K, tt), lambda i, lo, hi: (0, i), memory_space=pltpu.SMEM),
                      pl.BlockSpec((tt, ROW_CHUNKS, LANES), lambda i, lo, hi: (i, 0, 0))],
            out_specs=pl.BlockSpec(memory_space=pl.ANY),
            scratch_shapes=[pltpu.VMEM((ROW_CHUNKS, LANES), jnp.float32),
                            pltpu.SemaphoreType.DMA, pltpu.SemaphoreType.DMA]),
        out_shape=jax.ShapeDtypeStruct((n_rows, ROW_CHUNKS, LANES), jnp.float32),
        compiler_params=_params("arbitrary"),
        name="moe_dispatch",
    )(pad_lo, pad_hi, pos, h3)
    return xs.reshape(n_rows * ROW_CHUNKS, LANES)


def _silu_mul(gate_up):
    half = gate_up.shape[-1] // 2
    gate = gate_up[:, :half]
    up = gate_up[:, half:]
    return (gate / (1.0 + jnp.exp(-gate)) * up).astype(jnp.bfloat16)


def _expert_kernel(be_ref, nu_ref, x_ref, wgu_ref, wd_ref, y_ref):
    blk = EXPERT_BLOCK
    b = pl.program_id(0)

    @pl.when(b < nu_ref[0])
    def _():
        x = jnp.concatenate([x_ref[pl.ds(s, blk, stride=ROW_CHUNKS), :] for s in range(ROW_CHUNKS)],
                            axis=-1).astype(jnp.bfloat16)
        act = _silu_mul(jnp.dot(x, wgu_ref[0], preferred_element_type=jnp.float32))
        y = jnp.dot(act, wd_ref[0], preferred_element_type=jnp.float32)
        for s in range(ROW_CHUNKS):
            y_ref[pl.ds(s, blk, stride=ROW_CHUNKS), :] = y[:, s * LANES:(s + 1) * LANES]

    @pl.when(b >= nu_ref[0])
    def _():
        y_ref[...] = jnp.zeros_like(y_ref)


def _experts(block_expert, n_used, x_lin, wgu, wd):
    n_blocks = block_expert.shape[0]
    rows = EXPERT_BLOCK * ROW_CHUNKS
    return pl.pallas_call(
        _expert_kernel,
        grid_spec=pltpu.PrefetchScalarGridSpec(
            num_scalar_prefetch=2,
            grid=(n_blocks,),
            in_specs=[pl.BlockSpec((rows, LANES), lambda b, be, nu: (b, 0)),
                      pl.BlockSpec((1, D_MODEL, 2 * EXPERT_DIM), lambda b, be, nu: (be[b], 0, 0)),
                      pl.BlockSpec((1, EXPERT_DIM, D_MODEL), lambda b, be, nu: (be[b], 0, 0))],
            out_specs=pl.BlockSpec((rows, LANES), lambda b, be, nu: (b, 0))),
        out_shape=jax.ShapeDtypeStruct(x_lin.shape, jnp.float32),
        compiler_params=_params("arbitrary"),
        name="moe_experts",
    )(block_expert, n_used, x_lin, wgu, wd)


def _moe_out_kernel(pos_ref, posn_ref, hb_ref, h_ref, ys_ref, w_ref, wsgu_ref, wsd_ref, g_ref, b_ref,
                    of_ref, ob_ref, buf0, buf1, sem0, sem1, *, tm):
    step = pl.program_id(0)
    nstep = pl.num_programs(0)

    def gather(p_ref, off, buf, sem):
        def body(t, carry):
            for k in range(TOP_K):
                dst = pl.multiple_of((k * tm + t) * ROW_CHUNKS, ROW_CHUNKS)
                pltpu.make_async_copy(ys_ref.at[p_ref[k, off + t]], buf.at[pl.ds(dst, ROW_CHUNKS), :], sem).start()
            return carry

        lax.fori_loop(0, tm, body, 0)

    def wait_all(buf, sem):
        pltpu.make_async_copy(buf, buf, sem).wait()

    def compute(buf, r0):
        rows = slice(r0, r0 + tm)
        act = _silu_mul(jnp.dot(hb_ref[rows, :], wsgu_ref[...], preferred_element_type=jnp.float32))
        y = jnp.dot(act, wsd_ref[...], preferred_element_type=jnp.float32)
        w = w_ref[rows, :]
        cols = []
        for s in range(ROW_CHUNKS):
            acc = jnp.zeros((tm, LANES), jnp.float32)
            for k in range(TOP_K):
                acc = acc + w[:, k:k + 1] * buf[pl.ds(k * tm * ROW_CHUNKS + s, tm, stride=ROW_CHUNKS), :]
            cols.append(acc)
        y = y + jnp.concatenate(cols, axis=-1)
        out = _deepnorm(DEEPNORM_ALPHA * h_ref[rows, :] + y, g_ref[...], b_ref[...])
        of_ref[rows, :] = out
        ob_ref[rows, :] = out.astype(jnp.bfloat16)

    @pl.when(step == 0)
    def _():
        gather(pos_ref, 0, buf0, sem0)

    gather(pos_ref, tm, buf1, sem1)
    wait_all(buf0, sem0)
    compute(buf0, 0)

    @pl.when(step < nstep - 1)
    def _():
        gather(posn_ref, 0, buf0, sem0)

    wait_all(buf1, sem1)
    compute(buf1, tm)


def _moe_out(pos, hb, h, y_lin, w_tk, wsgu, wsd, g, b):
    tokens = h.shape[0]
    tm = 128
    nstep = tokens // (2 * tm)
    n_rows = y_lin.shape[0] // ROW_CHUNKS
    row = pl.BlockSpec((2 * tm, D_MODEL), lambda i: (i, 0))
    vec = pl.BlockSpec((1, D_MODEL), lambda i: (0, 0))
    return pl.pallas_call(
        functools.partial(_moe_out_kernel, tm=tm),
        grid=(nstep,),
        in_specs=[pl.BlockSpec((TOP_K, 2 * tm), lambda i: (0, i), memory_space=pltpu.SMEM),
                  pl.BlockSpec((TOP_K, 2 * tm), lambda i: (0, jnp.minimum(i + 1, nstep - 1)),
                               memory_space=pltpu.SMEM),
                  row, row,
                  pl.BlockSpec(memory_space=pl.ANY),
                  pl.BlockSpec((2 * tm, TOP_K), lambda i: (i, 0)),
                  pl.BlockSpec((D_MODEL, 2 * EXPERT_DIM), lambda i: (0, 0)),
                  pl.BlockSpec((EXPERT_DIM, D_MODEL), lambda i: (0, 0)),
                  vec, vec],
        out_specs=[row, row],
        out_shape=[jax.ShapeDtypeStruct((tokens, D_MODEL), jnp.float32),
                   jax.ShapeDtypeStruct((tokens, D_MODEL), jnp.bfloat16)],
        scratch_shapes=[pltpu.VMEM((TOP_K * tm * ROW_CHUNKS, LANES), jnp.float32),
                        pltpu.VMEM((TOP_K * tm * ROW_CHUNKS, LANES), jnp.float32),
                        pltpu.SemaphoreType.DMA, pltpu.SemaphoreType.DMA],
        compiler_params=_params("arbitrary"),
        name="moe_combine_deepnorm",
    )(pos, pos, hb, h, y_lin.reshape(n_rows, ROW_CHUNKS, LANES), w_tk, wsgu, wsd, g, b)


def _moe_layer(h, hb, h_lin, w_router, b_router, w_gate_up, w_down, ws_gate_up, ws_down, g, b):
    tokens = h.shape[0]
    n_assign = tokens * TOP_K
    n_blocks = n_assign // EXPERT_BLOCK + N_EXPERTS
    n_rows = n_blocks * EXPERT_BLOCK
    perm = (jnp.arange(N_EXPERTS) % N_GROUPS) * EXPERTS_PER_GROUP + jnp.arange(N_EXPERTS) // N_GROUPS
    top_idx, top_w, rank, cnt_perm = _router(h, w_router.T[perm], b_router[perm].reshape(N_EXPERTS, 1))

    counts = jnp.zeros((N_EXPERTS,), jnp.int32).at[perm].set(cnt_perm[:, 0].astype(jnp.int32))
    padded = (counts + EXPERT_BLOCK - 1) // EXPERT_BLOCK * EXPERT_BLOCK
    padded_end = jnp.cumsum(padded)
    padded_start = padded_end - padded
    experts = jnp.arange(N_EXPERTS, dtype=jnp.int32)[:, None, None]
    pos = rank + jnp.sum(jnp.where(top_idx[None] == experts, padded_start[:, None, None], 0), axis=0)
    block_row = jnp.arange(n_blocks, dtype=jnp.int32) * EXPERT_BLOCK
    block_expert = jnp.minimum(jnp.sum((padded_end[None, :] <= block_row[:, None]).astype(jnp.int32), axis=1),
                               N_EXPERTS - 1)
    n_used = (padded_end[-1:] // EXPERT_BLOCK).astype(jnp.int32)
    pad_lo = jnp.concatenate([padded_start + counts, padded_end[-1:]]).astype(jnp.int32)
    pad_hi = jnp.concatenate([padded_end, jnp.full((1,), n_rows)]).astype(jnp.int32)

    x_lin = _dispatch(pad_lo, pad_hi, pos, h_lin, n_rows)
    y_lin = _experts(block_expert, n_used, x_lin, w_gate_up.astype(jnp.bfloat16), w_down.astype(jnp.bfloat16))
    return _moe_out(pos, hb, h, y_lin, top_w.T, ws_gate_up.astype(jnp.bfloat16), ws_down.astype(jnp.bfloat16),
                    g.reshape(1, D_MODEL), b.reshape(1, D_MODEL))


def _q_scale(width, q_cols):
    return jnp.where(jnp.arange(width) < q_cols, HEAD_DIM ** -0.5, 1.0).astype(jnp.float32).reshape(1, width)


def _fox_mixer(hb, w_in, b_forget, batch, seq):
    width = 3 * D_MODEL
    qkv = _matmul(hb, w_in[:, :width].astype(jnp.bfloat16), _q_scale(width, D_MODEL), jnp.bfloat16,
                  1024, 512, "fox_in_proj")
    w_f = jnp.pad(w_in[:, width:], ((0, 0), (0, LANES - FOX_HEADS))).astype(jnp.bfloat16)
    f_logit = _matmul(hb, w_f, jnp.ones((1, LANES), jnp.float32), jnp.float32, 1024, LANES, "fox_gate_proj")
    b_pad = jnp.pad(b_forget, (0, LANES - FOX_HEADS)).reshape(1, LANES)
    c, ct = _forget_cumsum(f_logit, b_pad, batch, seq)
    return _fox_attention(qkv, c, ct, batch, seq)


def _rotary_tables(seq):
    half = ROT_DIM // 2
    inv_freq = ROPE_THETA ** (-jnp.arange(0, ROT_DIM, 2, dtype=jnp.float32) / ROT_DIM)
    ang = jnp.arange(seq, dtype=jnp.float32)[:, None] * inv_freq[None, :]
    cos, sin = jnp.cos(ang), jnp.sin(ang)
    zeros = jnp.zeros((seq, HEAD_DIM - ROT_DIM), jnp.float32)
    zh = jnp.zeros((seq, half), jnp.float32)
    cos_t = jnp.concatenate([cos, cos, jnp.ones_like(zeros)], axis=1)
    sin_a = jnp.concatenate([zh, sin, zeros], axis=1)
    sin_b = jnp.concatenate([-sin, zh, zeros], axis=1)
    return cos_t, sin_a, sin_b


def _diff_mixer(hb, layer_idx, w_in, lambda_qk, subln_g, tables, batch, seq):
    width = 3 * D_MODEL
    qkv = _matmul(hb, w_in.astype(jnp.bfloat16), _q_scale(width, D_MODEL), jnp.bfloat16, 1024, 512,
                  "diff_in_proj")
    lam_init = 0.8 - 0.6 * math.exp(-0.3 * layer_idx)
    lq = lambda_qk.astype(jnp.float32)
    lam = (jnp.exp(jnp.sum(lq[0] * lq[1])) - jnp.exp(jnp.sum(lq[2] * lq[3])) + lam_init).reshape(1)
    return _diff_attention(qkv, lam, *tables, subln_g.reshape(1, 2 * HEAD_DIM), lam_init, batch, seq)


def kernel(x, fox_w_in_0, fox_b_forget_0, fox_w_out_0, ln_mix_g_0, ln_mix_b_0, moe_w_router_0, moe_b_router_0, moe_w_gate_up_0, moe_w_down_0, moe_ws_gate_up_0, moe_ws_down_0, ln_ffn_g_0, ln_ffn_b_0, diff_w_in_1, diff_lambda_qk_1, diff_subln_g_1, diff_w_out_1, ln_mix_g_1, ln_mix_b_1, moe_w_router_1, moe_b_router_1, moe_w_gate_up_1, moe_w_down_1, moe_ws_gate_up_1, moe_ws_down_1, ln_ffn_g_1, ln_ffn_b_1, fox_w_in_2, fox_b_forget_2, fox_w_out_2, ln_mix_g_2, ln_mix_b_2, moe_w_router_2, moe_b_router_2, moe_w_gate_up_2, moe_w_down_2, moe_ws_gate_up_2, moe_ws_down_2, ln_ffn_g_2, ln_ffn_b_2, diff_w_in_3, diff_lambda_qk_3, diff_subln_g_3, diff_w_out_3, ln_mix_g_3, ln_mix_b_3, moe_w_router_3, moe_b_router_3, moe_w_gate_up_3, moe_w_down_3, moe_ws_gate_up_3, moe_ws_down_3, ln_ffn_g_3, ln_ffn_b_3):
    batch, seq, _ = x.shape
    tokens = batch * seq
    mix = [(fox_w_in_0, fox_b_forget_0, fox_w_out_0),
           (diff_w_in_1, diff_lambda_qk_1, diff_subln_g_1, diff_w_out_1),
           (fox_w_in_2, fox_b_forget_2, fox_w_out_2),
           (diff_w_in_3, diff_lambda_qk_3, diff_subln_g_3, diff_w_out_3)]
    norm_mix = [(ln_mix_g_0, ln_mix_b_0), (ln_mix_g_1, ln_mix_b_1), (ln_mix_g_2, ln_mix_b_2), (ln_mix_g_3, ln_mix_b_3)]
    moe = [(moe_w_router_0, moe_b_router_0, moe_w_gate_up_0, moe_w_down_0, moe_ws_gate_up_0, moe_ws_down_0),
           (moe_w_router_1, moe_b_router_1, moe_w_gate_up_1, moe_w_down_1, moe_ws_gate_up_1, moe_ws_down_1),
           (moe_w_router_2, moe_b_router_2, moe_w_gate_up_2, moe_w_down_2, moe_ws_gate_up_2, moe_ws_down_2),
           (moe_w_router_3, moe_b_router_3, moe_w_gate_up_3, moe_w_down_3, moe_ws_gate_up_3, moe_ws_down_3)]
    norm_ffn = [(ln_ffn_g_0, ln_ffn_b_0), (ln_ffn_g_1, ln_ffn_b_1), (ln_ffn_g_2, ln_ffn_b_2), (ln_ffn_g_3, ln_ffn_b_3)]

    tables = _rotary_tables(seq)
    h = x.reshape(tokens, D_MODEL)
    hb = h.astype(jnp.bfloat16)
    for i in range(DEPTH):
        if i % 2 == 0:
            w_in, b_forget, w_out = mix[i]
            o = _fox_mixer(hb, w_in, b_forget, batch, seq)
        else:
            w_in, lambda_qk, subln_g, w_out = mix[i]
            o = _diff_mixer(hb, i, w_in, lambda_qk, subln_g, tables, batch, seq)
        g, b = norm_mix[i]
        h, hb, h_lin = _proj_ln(o, w_out.astype(jnp.bfloat16), h, g.reshape(1, D_MODEL), b.reshape(1, D_MODEL))
        h, hb = _moe_layer(h, hb, h_lin, *moe[i], *norm_ffn[i])
    return h.reshape(batch, seq, D_MODEL)
```

```python
import functools
import math

import jax
import jax.numpy as jnp
from jax import lax
from jax.experimental import pallas as pl
from jax.experimental.pallas import tpu as pltpu

D_MODEL = 2048
DEPTH = 4
HEAD_DIM = 128
LANES = 128
ROW_CHUNKS = D_MODEL // (2 * LANES)
FOX_HEADS = D_MODEL // HEAD_DIM
DIFF_HEADS = D_MODEL // (2 * HEAD_DIM)
ROT_DIM = HEAD_DIM // 4
ROPE_THETA = 500000.0
N_EXPERTS = 64
TOP_K = 8
N_GROUPS = 8
TOPK_GROUPS = 4
EXPERTS_PER_GROUP = N_EXPERTS // N_GROUPS
EXPERT_DIM = D_MODEL // 4
ROUTED_SCALE = 2.5
DEEPNORM_ALPHA = (2 * DEPTH) ** 0.25
LN_EPS = 1e-5
SUBLN_EPS = 1e-5

ATT_BLOCK = 256
EXPERT_BLOCK = 256
VMEM_LIMIT = 56 * 1024 * 1024

_HIGHEST = lax.Precision.HIGHEST
_NT = (((1,), (1,)), ((), ()))


def _params(*sem):
    return pltpu.CompilerParams(dimension_semantics=sem, vmem_limit_bytes=VMEM_LIMIT)


def _pack_rows(ref, row0, rows, x):
    half = D_MODEL // 2
    for s in range(ROW_CHUNKS):
        lo = x[:, s * LANES:(s + 1) * LANES].astype(jnp.bfloat16).astype(jnp.float32)
        hi = x[:, half + s * LANES:half + (s + 1) * LANES].astype(jnp.bfloat16).astype(jnp.float32)
        word = lax.bitcast_convert_type(hi, jnp.uint32) | (lax.bitcast_convert_type(lo, jnp.uint32) >> 16)
        ref[pl.ds(row0 * ROW_CHUNKS + s, rows, stride=ROW_CHUNKS), :] = word


def _unpack_chunk(ref, row0, rows, s):
    u = ref[pl.ds(row0 * ROW_CHUNKS + s, rows, stride=ROW_CHUNKS), :]
    lo = lax.bitcast_convert_type(u << 16, jnp.float32)
    hi = lax.bitcast_convert_type(u & jnp.uint32(0xFFFF0000), jnp.float32)
    return lo, hi


def _mm_kernel(a_ref, b_ref, s_ref, o_ref):
    acc = jnp.dot(a_ref[...], b_ref[...], preferred_element_type=jnp.float32)
    o_ref[...] = (acc * s_ref[...]).astype(o_ref.dtype)


def _matmul(a, b, col_scale, out_dtype, tm, tn, name):
    m, k = a.shape
    n = b.shape[1]
    return pl.pallas_call(
        _mm_kernel,
        grid=(m // tm, n // tn),
        in_specs=[pl.BlockSpec((tm, k), lambda i, j: (i, 0)),
                  pl.BlockSpec((k, tn), lambda i, j: (0, j)),
                  pl.BlockSpec((1, tn), lambda i, j: (0, j))],
        out_specs=pl.BlockSpec((tm, tn), lambda i, j: (i, j)),
        out_shape=jax.ShapeDtypeStruct((m, n), out_dtype),
        compiler_params=_params("parallel", "parallel"),
        name=name,
    )(a, b, col_scale)


def _fgate_kernel(fl_ref, b_ref, c_ref, ct_ref, *, seq, chunk):
    x = fl_ref[...] + b_ref[...]
    ls = -(jnp.maximum(-x, 0.0) + jnp.log1p(jnp.exp(-jnp.abs(x))))
    r = lax.broadcasted_iota(jnp.int32, (chunk, chunk), 0)
    c = lax.broadcasted_iota(jnp.int32, (chunk, chunk), 1)
    tri = (r >= c).astype(jnp.float32)
    carry = jnp.zeros((1, LANES), jnp.float32)
    for ch in range(seq // chunk):
        blk = jnp.dot(tri, ls[ch * chunk:(ch + 1) * chunk], precision=_HIGHEST,
                      preferred_element_type=jnp.float32) + carry
        carry = blk[chunk - 1:chunk, :]
        c_ref[0, ch * chunk:(ch + 1) * chunk, :] = blk
        ct_ref[0, :, ch * chunk:(ch + 1) * chunk] = blk.T[:FOX_HEADS]


def _forget_cumsum(f_logit, b_pad, batch, seq):
    chunk = 256
    return pl.pallas_call(
        functools.partial(_fgate_kernel, seq=seq, chunk=chunk),
        grid=(batch,),
        in_specs=[pl.BlockSpec((seq, LANES), lambda b: (b, 0)),
                  pl.BlockSpec((1, LANES), lambda b: (0, 0))],
        out_specs=[pl.BlockSpec((1, seq, LANES), lambda b: (b, 0, 0)),
                   pl.BlockSpec((1, FOX_HEADS, seq), lambda b: (b, 0, 0))],
        out_shape=[jax.ShapeDtypeStruct((batch, seq, LANES), jnp.float32),
                   jax.ShapeDtypeStruct((batch, FOX_HEADS, seq), jnp.float32)],
        compiler_params=_params("parallel"),
        name="forget_cumsum",
    )(f_logit, b_pad)


def _causal_attend(q, k_ref, v_ref, i, row_bias, col_bias):
    blk = ATT_BLOCK
    lo = i * blk
    s_d = lax.dot_general(q, k_ref[lo:lo + blk, :], _NT, preferred_element_type=jnp.float32)
    if row_bias is not None:
        s_d = s_d + row_bias - col_bias[:, lo:lo + blk]
    r = lax.broadcasted_iota(jnp.int32, (blk, blk), 0)
    c = lax.broadcasted_iota(jnp.int32, (blk, blk), 1)
    s_d = jnp.where(c <= r, s_d, -jnp.inf)
    m = jnp.max(s_d, axis=-1, keepdims=True)
    if i > 0:
        s_o = lax.dot_general(q, k_ref[0:lo, :], _NT, preferred_element_type=jnp.float32)
        if row_bias is not None:
            s_o = s_o + row_bias - col_bias[:, 0:lo]
        m = jnp.maximum(m, jnp.max(s_o, axis=-1, keepdims=True))
    p_d = jnp.exp(s_d - m)
    l = jnp.sum(p_d, axis=-1, keepdims=True)
    acc = jnp.dot(p_d.astype(jnp.bfloat16), v_ref[lo:lo + blk, :], preferred_element_type=jnp.float32)
    if i > 0:
        p_o = jnp.exp(s_o - m)
        l = l + jnp.sum(p_o, axis=-1, keepdims=True)
        acc = acc + jnp.dot(p_o.astype(jnp.bfloat16), v_ref[0:lo, :], preferred_element_type=jnp.float32)
    return acc / l


def _fox_attn_kernel(q_ref, k_ref, v_ref, c_ref, ct_ref, o_ref, *, seq):
    h = pl.program_id(1)
    c_all = c_ref[0]
    lane = lax.broadcasted_iota(jnp.int32, c_all.shape, 1)
    c_row = jnp.sum(jnp.where(lane == h, c_all, 0.0), axis=1, keepdims=True)
    ct_all = ct_ref[0]
    sub = lax.broadcasted_iota(jnp.int32, ct_all.shape, 0)
    c_col = jnp.sum(jnp.where(sub == h, ct_all, 0.0), axis=0, keepdims=True)
    for i in range(seq // ATT_BLOCK):
        lo = i * ATT_BLOCK
        q = q_ref[lo:lo + ATT_BLOCK, :]
        out = _causal_attend(q, k_ref, v_ref, i, c_row[lo:lo + ATT_BLOCK, :], c_col)
        o_ref[lo:lo + ATT_BLOCK, :] = out.astype(o_ref.dtype)


def _fox_attention(qkv, c, ct, batch, seq):
    tokens = batch * seq
    kern = functools.partial(_fox_attn_kernel, seq=seq)
    return pl.pallas_call(
        kern,
        grid=(batch, FOX_HEADS),
        in_specs=[pl.BlockSpec((seq, HEAD_DIM), lambda b, h: (b, h)),
                  pl.BlockSpec((seq, HEAD_DIM), lambda b, h: (b, FOX_HEADS + h)),
                  pl.BlockSpec((seq, HEAD_DIM), lambda b, h: (b, 2 * FOX_HEADS + h)),
                  pl.BlockSpec((1, seq, LANES), lambda b, h: (b, 0, 0)),
                  pl.BlockSpec((1, FOX_HEADS, seq), lambda b, h: (b, 0, 0))],
        out_specs=pl.BlockSpec((seq, HEAD_DIM), lambda b, h: (b, h)),
        out_shape=jax.ShapeDtypeStruct((tokens, D_MODEL), jnp.bfloat16),
        compiler_params=_params("parallel", "parallel"),
        name="fox_attention",
    )(qkv, qkv, qkv, c, ct)


def _diff_attn_kernel(lam_ref, q1_ref, q2_ref, k1_ref, k2_ref, v_ref, cos_ref, sa_ref, sb_ref, g_ref,
                      o_ref, q1s, q2s, k1s, k2s, *, seq, out_scale):
    cos = cos_ref[...]
    sa = sa_ref[...]
    sb = sb_ref[...]
    half = ROT_DIM // 2
    for src, dst in ((q1_ref, q1s), (q2_ref, q2s), (k1_ref, k1s), (k2_ref, k2s)):
        x = src[...].astype(jnp.float32)
        rot = x * cos + pltpu.roll(x, half, axis=1) * sa + pltpu.roll(x, HEAD_DIM - half, axis=1) * sb
        dst[...] = rot.astype(jnp.bfloat16)
    lam = lam_ref[0]
    g = g_ref[...] * out_scale
    for i in range(seq // ATT_BLOCK):
        lo = i * ATT_BLOCK
        a1 = _causal_attend(q1s[lo:lo + ATT_BLOCK, :], k1s, v_ref, i, None, None)
        a2 = _causal_attend(q2s[lo:lo + ATT_BLOCK, :], k2s, v_ref, i, None, None)
        o = a1 - lam * a2
        o = o * lax.rsqrt(jnp.mean(jnp.square(o), axis=-1, keepdims=True) + SUBLN_EPS)
        o_ref[lo:lo + ATT_BLOCK, :] = (o * g).astype(o_ref.dtype)


def _diff_attention(qkv, lam, cos_t, sin_a, sin_b, subln_g, lam_init, batch, seq):
    tokens = batch * seq
    kern = functools.partial(_diff_attn_kernel, seq=seq, out_scale=1.0 - lam_init)
    kv_off = 2 * DIFF_HEADS
    v_off = 2 * DIFF_HEADS
    tab = pl.BlockSpec((seq, HEAD_DIM), lambda b, h: (0, 0))
    return pl.pallas_call(
        kern,
        grid=(batch, DIFF_HEADS),
        in_specs=[pl.BlockSpec(memory_space=pltpu.SMEM),
                  pl.BlockSpec((seq, HEAD_DIM), lambda b, h: (b, 2 * h)),
                  pl.BlockSpec((seq, HEAD_DIM), lambda b, h: (b, 2 * h + 1)),
                  pl.BlockSpec((seq, HEAD_DIM), lambda b, h: (b, kv_off + 2 * h)),
                  pl.BlockSpec((seq, HEAD_DIM), lambda b, h: (b, kv_off + 2 * h + 1)),
                  pl.BlockSpec((seq, 2 * HEAD_DIM), lambda b, h: (b, v_off + h)),
                  tab, tab, tab,
                  pl.BlockSpec((1, 2 * HEAD_DIM), lambda b, h: (0, 0))],
        out_specs=pl.BlockSpec((seq, 2 * HEAD_DIM), lambda b, h: (b, h)),
        out_shape=jax.ShapeDtypeStruct((tokens, D_MODEL), jnp.bfloat16),
        scratch_shapes=[pltpu.VMEM((seq, HEAD_DIM), jnp.bfloat16)] * 4,
        compiler_params=_params("parallel", "parallel"),
        name="diff_attention",
    )(lam, qkv, qkv, qkv, qkv, qkv, cos_t, sin_a, sin_b, subln_g)


def _deepnorm(z, g, b):
    mu = jnp.mean(z, axis=-1, keepdims=True)
    zc = z - mu
    var = jnp.mean(zc * zc, axis=-1, keepdims=True)
    return zc * lax.rsqrt(var + LN_EPS) * g + b


def _proj_ln_kernel(a_ref, w_ref, h_ref, g_ref, b_ref, of_ref, ob_ref, ol_ref, *, tm):
    y = jnp.dot(a_ref[...], w_ref[...], preferred_element_type=jnp.float32)
    out = _deepnorm(DEEPNORM_ALPHA * h_ref[...] + y, g_ref[...], b_ref[...])
    of_ref[...] = out
    ob_ref[...] = out.astype(jnp.bfloat16)
    _pack_rows(ol_ref, 0, tm, out)


def _proj_ln(a, w, h, g, b):
    tokens = a.shape[0]
    tm = 256
    row = pl.BlockSpec((tm, D_MODEL), lambda i: (i, 0))
    vec = pl.BlockSpec((1, D_MODEL), lambda i: (0, 0))
    return pl.pallas_call(
        functools.partial(_proj_ln_kernel, tm=tm),
        grid=(tokens // tm,),
        in_specs=[row, pl.BlockSpec((D_MODEL, D_MODEL), lambda i: (0, 0)), row, vec, vec],
        out_specs=[row, row, pl.BlockSpec((tm * ROW_CHUNKS, LANES), lambda i: (i, 0))],
        out_shape=[jax.ShapeDtypeStruct((tokens, D_MODEL), jnp.float32),
                   jax.ShapeDtypeStruct((tokens, D_MODEL), jnp.bfloat16),
                   jax.ShapeDtypeStruct((tokens * ROW_CHUNKS, LANES), jnp.uint32)],
        compiler_params=_params("parallel"),
        name="outproj_deepnorm",
    )(a, w, h, g, b)


def _router_kernel(h_ref, wt_ref, bias_ref, idx_ref, w_ref, rank_ref, cnt_ref, carry_ref, *, tm):
    step = pl.program_id(0)

    @pl.when(step == 0)
    def _():
        carry_ref[...] = jnp.zeros_like(carry_ref)

    logits = lax.dot_general(wt_ref[...], h_ref[...], _NT, precision=_HIGHEST,
                             preferred_element_type=jnp.float32)
    scores = 1.0 / (1.0 + jnp.exp(-logits))
    biased = scores + bias_ref[...]
    npg = EXPERTS_PER_GROUP
    group = lax.broadcasted_iota(jnp.int32, (N_GROUPS, tm), 0)
    s = [scores[j * N_GROUPS:(j + 1) * N_GROUPS] for j in range(npg)]
    bz = [biased[j * N_GROUPS:(j + 1) * N_GROUPS] for j in range(npg)]
    eid = [group * npg + j for j in range(npg)]
    neg = -jnp.inf

    m1 = functools.reduce(jnp.maximum, bz)
    j1 = functools.reduce(jnp.minimum, [jnp.where(bz[j] == m1, j, npg) for j in range(npg)])
    m2 = functools.reduce(jnp.maximum, [jnp.where(j1 == j, neg, bz[j]) for j in range(npg)])
    gs = m1 + m2
    beaten = jnp.zeros((N_GROUPS, tm), jnp.int32)
    for r in range(1, N_GROUPS):
        og = pltpu.roll(gs, r, axis=0)
        oi = pltpu.roll(group, r, axis=0)
        wins = jnp.where(og > gs, 1, jnp.where(og == gs, jnp.where(oi < group, 1, 0), 0))
        beaten = beaten + wins
    keep = beaten < TOPK_GROUPS
    mb = [jnp.where(keep, bz[j], neg) for j in range(npg)]

    sel_idx, sel_w = [], []
    chosen = [jnp.zeros((N_GROUPS, tm), jnp.float32) for _ in range(npg)]
    for _ in range(TOP_K):
        m = jnp.max(functools.reduce(jnp.maximum, mb), axis=0, keepdims=True)
        cand = functools.reduce(jnp.minimum, [jnp.where(mb[j] == m, eid[j], N_EXPERTS) for j in range(npg)])
        ik = jnp.min(cand, axis=0, keepdims=True)
        hit = [eid[j] == ik for j in range(npg)]
        wk = functools.reduce(jnp.add, [jnp.where(hit[j], s[j], 0.0) for j in range(npg)])
        sel_idx.append(ik)
        sel_w.append(jnp.sum(wk, axis=0, keepdims=True))
        mb = [jnp.where(hit[j], neg, mb[j]) for j in range(npg)]
        chosen = [jnp.where(hit[j], 1.0, chosen[j]) for j in range(npg)]
    wsum = functools.reduce(jnp.add, sel_w)
    idx_ref[...] = jnp.concatenate(sel_idx, axis=0)
    w_ref[...] = jnp.concatenate(sel_w, axis=0) / wsum * ROUTED_SCALE

    onehot = jnp.concatenate(chosen, axis=0)
    tr = lax.broadcasted_iota(jnp.int32, (tm, tm), 0)
    tc = lax.broadcasted_iota(jnp.int32, (tm, tm), 1)
    before = (tr < tc).astype(jnp.bfloat16)
    excl = jnp.dot(onehot.astype(jnp.bfloat16), before, preferred_element_type=jnp.float32)
    base = excl + carry_ref[:, 0:1]
    bs = [base[j * N_GROUPS:(j + 1) * N_GROUPS] for j in range(npg)]
    ranks = []
    for k in range(TOP_K):
        rk = functools.reduce(jnp.add, [jnp.where(eid[j] == sel_idx[k], bs[j], 0.0) for j in range(npg)])
        ranks.append(jnp.sum(rk, axis=0, keepdims=True))
    rank_ref[...] = jnp.concatenate(ranks, axis=0).astype(jnp.int32)
    carry_ref[...] = carry_ref[...] + jnp.sum(onehot, axis=1, keepdims=True)
    cnt_ref[...] = carry_ref[...]


def _router(h, wt_perm, bias_perm):
    tokens = h.shape[0]
    tm = 512
    sel = pl.BlockSpec((TOP_K, tm), lambda i: (0, i))
    return pl.pallas_call(
        functools.partial(_router_kernel, tm=tm),
        grid=(tokens // tm,),
        in_specs=[pl.BlockSpec((tm, D_MODEL), lambda i: (i, 0)),
                  pl.BlockSpec((N_EXPERTS, D_MODEL), lambda i: (0, 0)),
                  pl.BlockSpec((N_EXPERTS, 1), lambda i: (0, 0))],
        out_specs=[sel, sel, sel, pl.BlockSpec((N_EXPERTS, LANES), lambda i: (0, 0))],
        out_shape=[jax.ShapeDtypeStruct((TOP_K, tokens), jnp.int32),
                   jax.ShapeDtypeStruct((TOP_K, tokens), jnp.float32),
                   jax.ShapeDtypeStruct((TOP_K, tokens), jnp.int32),
                   jax.ShapeDtypeStruct((N_EXPERTS, LANES), jnp.float32)],
        scratch_shapes=[pltpu.VMEM((N_EXPERTS, LANES), jnp.float32)],
        compiler_params=_params("arbitrary"),
        name="moe_router",
    )(h, wt_perm, bias_perm)


def _dispatch_kernel(lo_ref, hi_ref, pos_ref, h3_ref, xs_ref, zbuf, sem, zsem, *, tt):
    step = pl.program_id(0)

    def issue(t, carry):
        for k in range(TOP_K):
            pltpu.make_async_copy(h3_ref.at[t], xs_ref.at[pos_ref[k, t]], sem).start(priority=k % 2)
        return carry

    lax.fori_loop(0, tt, issue, 0)

    @pl.when(step == 0)
    def _():
        zbuf[...] = jnp.zeros_like(zbuf)

        def per_range(e, carry):
            def fill(r, c2):
                pltpu.make_async_copy(zbuf, xs_ref.at[r], zsem).start()
                return c2

            def drain(r, c2):
                pltpu.make_async_copy(zbuf, xs_ref.at[0], zsem).wait()
                return c2

            lax.fori_loop(lo_ref[e], hi_ref[e], fill, 0)
            lax.fori_loop(lo_ref[e], hi_ref[e], drain, 0)
            return carry

        lax.fori_loop(0, N_EXPERTS + 1, per_range, 0)

    rows = pl.ds(0, tt * TOP_K)
    pltpu.make_async_copy(xs_ref.at[rows], xs_ref.at[rows], sem).wait()


def _dispatch(pad_lo, pad_hi, pos, h_lin, n_rows):
    tokens = h_lin.shape[0] // ROW_CHUNKS
    tt = 256
    h3 = h_lin.reshape(tokens, ROW_CHUNKS, LANES)
    xs = pl.pallas_call(
        functools.partial(_dispatch_kernel, tt=tt),
        grid_spec=pltpu.PrefetchScalarGridSpec(
            num_scalar_prefetch=2,
            grid=(tokens // tt,),
            in_specs=[pl.BlockSpec((TOP_K, tt), lambda i, lo, hi: (0, i), memory_space=pltpu.SMEM),
                      pl.BlockSpec((tt, ROW_CHUNKS, LANES), lambda i, lo, hi: (i, 0, 0))],
            out_specs=pl.BlockSpec(memory_space=pl.ANY),
            scratch_shapes=[pltpu.VMEM((ROW_CHUNKS, LANES), jnp.uint32),
                            pltpu.SemaphoreType.DMA, pltpu.SemaphoreType.DMA]),
        out_shape=jax.ShapeDtypeStruct((n_rows, ROW_CHUNKS, LANES), jnp.uint32),
        compiler_params=_params("arbitrary"),
        name="moe_dispatch",
    )(pad_lo, pad_hi, pos, h3)
    return xs.reshape(n_rows * ROW_CHUNKS, LANES)


def _silu_mul(gate_up):
    half = gate_up.shape[-1] // 2
    gate = gate_up[:, :half]
    up = gate_up[:, half:]
    return (gate / (1.0 + jnp.exp(-gate)) * up).astype(jnp.bfloat16)


def _expert_kernel(be_ref, nu_ref, x_ref, wgu_ref, wd_ref, y_ref, wgu_b, wd_b):
    b = pl.program_id(0)
    active = b < nu_ref[0]
    expert = be_ref[b]
    prev = be_ref[jnp.maximum(b - 1, 0)]

    @pl.when(active & ((b == 0) | (expert != prev)))
    def _():
        wgu_b[...] = wgu_ref[0].astype(jnp.bfloat16)
        wd_b[...] = wd_ref[0].astype(jnp.bfloat16)

    @pl.when(active)
    def _():
        chunks = [_unpack_chunk(x_ref, 0, EXPERT_BLOCK, s) for s in range(ROW_CHUNKS)]
        x = jnp.concatenate([lo for lo, _ in chunks] + [hi for _, hi in chunks],
                            axis=-1).astype(jnp.bfloat16)
        act = _silu_mul(jnp.dot(x, wgu_b[...], preferred_element_type=jnp.float32))
        y = jnp.dot(act, wd_b[...], preferred_element_type=jnp.float32)
        _pack_rows(y_ref, 0, EXPERT_BLOCK, y)

    @pl.when(jnp.logical_not(active))
    def _():
        y_ref[...] = jnp.zeros_like(y_ref)


def _experts(block_expert, n_used, x_lin, wgu, wd):
    n_blocks = block_expert.shape[0]
    rows = EXPERT_BLOCK * ROW_CHUNKS
    return pl.pallas_call(
        _expert_kernel,
        grid_spec=pltpu.PrefetchScalarGridSpec(
            num_scalar_prefetch=2,
            grid=(n_blocks,),
            in_specs=[pl.BlockSpec((rows, LANES), lambda b, be, nu: (b, 0)),
                      pl.BlockSpec((1, D_MODEL, 2 * EXPERT_DIM), lambda b, be, nu: (be[b], 0, 0)),
                      pl.BlockSpec((1, EXPERT_DIM, D_MODEL), lambda b, be, nu: (be[b], 0, 0))],
            out_specs=pl.BlockSpec((rows, LANES), lambda b, be, nu: (b, 0)),
            scratch_shapes=[pltpu.VMEM((D_MODEL, 2 * EXPERT_DIM), jnp.bfloat16),
                            pltpu.VMEM((EXPERT_DIM, D_MODEL), jnp.bfloat16)]),
        out_shape=jax.ShapeDtypeStruct(x_lin.shape, jnp.uint32),
        compiler_params=_params("arbitrary"),
        name="moe_experts",
    )(block_expert, n_used, x_lin, wgu, wd)


def _moe_out_kernel(pos_ref, posn_ref, hb_ref, h_ref, ys_ref, w_ref, wsgu_ref, wsd_ref, g_ref, b_ref,
                    of_ref, ob_ref, buf0, buf1, sem0, sem1, *, tm):
    step = pl.program_id(0)
    nstep = pl.num_programs(0)

    def gather(p_ref, off, buf, sem):
        def body(t, carry):
            for k in range(TOP_K):
                dst = pl.multiple_of((k * tm + t) * ROW_CHUNKS, ROW_CHUNKS)
                pltpu.make_async_copy(ys_ref.at[p_ref[k, off + t]], buf.at[pl.ds(dst, ROW_CHUNKS), :], sem).start()
            return carry

        lax.fori_loop(0, tm, body, 0)

    def wait_all(buf, sem):
        pltpu.make_async_copy(buf, buf, sem).wait()

    def compute(buf, r0):
        rows = slice(r0, r0 + tm)
        act = _silu_mul(jnp.dot(hb_ref[rows, :], wsgu_ref[...], preferred_element_type=jnp.float32))
        y = jnp.dot(act, wsd_ref[...], preferred_element_type=jnp.float32)
        w = w_ref[rows, :]
        wk = [jnp.broadcast_to(w[:, k:k + 1], (tm, LANES)) for k in range(TOP_K)]
        acc_lo, acc_hi = [], []
        for s in range(ROW_CHUNKS):
            a_lo = jnp.zeros((tm, LANES), jnp.float32)
            a_hi = jnp.zeros((tm, LANES), jnp.float32)
            for k in range(TOP_K):
                lo, hi = _unpack_chunk(buf, k * tm, tm, s)
                a_lo = a_lo + wk[k] * lo
                a_hi = a_hi + wk[k] * hi
            acc_lo.append(a_lo)
            acc_hi.append(a_hi)
        y = y + jnp.concatenate(acc_lo + acc_hi, axis=-1)
        out = _deepnorm(DEEPNORM_ALPHA * h_ref[rows, :] + y, g_ref[...], b_ref[...])
        of_ref[rows, :] = out
        ob_ref[rows, :] = out.astype(jnp.bfloat16)

    @pl.when(step == 0)
    def _():
        gather(pos_ref, 0, buf0, sem0)

    gather(pos_ref, tm, buf1, sem1)
    wait_all(buf0, sem0)
    compute(buf0, 0)

    @pl.when(step < nstep - 1)
    def _():
        gather(posn_ref, 0, buf0, sem0)

    wait_all(buf1, sem1)
    compute(buf1, tm)


def _moe_out(pos, hb, h, y_lin, w_tk, wsgu, wsd, g, b):
    tokens = h.shape[0]
    tm = 128
    nstep = tokens // (2 * tm)
    n_rows = y_lin.shape[0] // ROW_CHUNKS
    row = pl.BlockSpec((2 * tm, D_MODEL), lambda i: (i, 0))
    vec = pl.BlockSpec((1, D_MODEL), lambda i: (0, 0))
    return pl.pallas_call(
        functools.partial(_moe_out_kernel, tm=tm),
        grid=(nstep,),
        in_specs=[pl.BlockSpec((TOP_K, 2 * tm), lambda i: (0, i), memory_space=pltpu.SMEM),
                  pl.BlockSpec((TOP_K, 2 * tm), lambda i: (0, jnp.minimum(i + 1, nstep - 1)),
                               memory_space=pltpu.SMEM),
                  row, row,
                  pl.BlockSpec(memory_space=pl.ANY),
                  pl.BlockSpec((2 * tm, TOP_K), lambda i: (i, 0)),
                  pl.BlockSpec((D_MODEL, 2 * EXPERT_DIM), lambda i: (0, 0)),
                  pl.BlockSpec((EXPERT_DIM, D_MODEL), lambda i: (0, 0)),
                  vec, vec],
        out_specs=[row, row],
        out_shape=[jax.ShapeDtypeStruct((tokens, D_MODEL), jnp.float32),
                   jax.ShapeDtypeStruct((tokens, D_MODEL), jnp.bfloat16)],
        scratch_shapes=[pltpu.VMEM((TOP_K * tm * ROW_CHUNKS, LANES), jnp.uint32),
                        pltpu.VMEM((TOP_K * tm * ROW_CHUNKS, LANES), jnp.uint32),
                        pltpu.SemaphoreType.DMA, pltpu.SemaphoreType.DMA],
        compiler_params=_params("arbitrary"),
        name="moe_combine_deepnorm",
    )(pos, pos, hb, h, y_lin.reshape(n_rows, ROW_CHUNKS, LANES), w_tk, wsgu, wsd, g, b)


def _moe_layer(h, hb, h_lin, w_router, b_router, w_gate_up, w_down, ws_gate_up, ws_down, g, b):
    tokens = h.shape[0]
    n_assign = tokens * TOP_K
    n_blocks = n_assign // EXPERT_BLOCK + N_EXPERTS
    n_rows = n_blocks * EXPERT_BLOCK
    perm = (jnp.arange(N_EXPERTS) % N_GROUPS) * EXPERTS_PER_GROUP + jnp.arange(N_EXPERTS) // N_GROUPS
    top_idx, top_w, rank, cnt_perm = _router(h, w_router.T[perm], b_router[perm].reshape(N_EXPERTS, 1))

    counts = jnp.zeros((N_EXPERTS,), jnp.int32).at[perm].set(cnt_perm[:, 0].astype(jnp.int32))
    padded = (counts + EXPERT_BLOCK - 1) // EXPERT_BLOCK * EXPERT_BLOCK
    padded_end = jnp.cumsum(padded)
    padded_start = padded_end - padded
    experts = jnp.arange(N_EXPERTS, dtype=jnp.int32)[:, None, None]
    pos = rank + jnp.sum(jnp.where(top_idx[None] == experts, padded_start[:, None, None], 0), axis=0)
    block_row = jnp.arange(n_blocks, dtype=jnp.int32) * EXPERT_BLOCK
    block_expert = jnp.minimum(jnp.sum((padded_end[None, :] <= block_row[:, None]).astype(jnp.int32), axis=1),
                               N_EXPERTS - 1)
    n_used = (padded_end[-1:] // EXPERT_BLOCK).astype(jnp.int32)
    pad_lo = jnp.concatenate([padded_start + counts, padded_end[-1:]]).astype(jnp.int32)
    pad_hi = jnp.concatenate([padded_end, jnp.full((1,), n_rows)]).astype(jnp.int32)

    x_lin = _dispatch(pad_lo, pad_hi, pos, h_lin, n_rows)
    y_lin = _experts(block_expert, n_used, x_lin, w_gate_up, w_down)
    return _moe_out(pos, hb, h, y_lin, top_w.T, ws_gate_up.astype(jnp.bfloat16), ws_down.astype(jnp.bfloat16),
                    g.reshape(1, D_MODEL), b.reshape(1, D_MODEL))


def _q_scale(width, q_cols):
    return jnp.where(jnp.arange(width) < q_cols, HEAD_DIM ** -0.5, 1.0).astype(jnp.float32).reshape(1, width)


def _fox_mixer(hb, w_in, b_forget, batch, seq):
    width = 3 * D_MODEL
    qkv = _matmul(hb, w_in[:, :width].astype(jnp.bfloat16), _q_scale(width, D_MODEL), jnp.bfloat16,
                  1024, 512, "fox_in_proj")
    w_f = jnp.pad(w_in[:, width:], ((0, 0), (0, LANES - FOX_HEADS))).astype(jnp.bfloat16)
    f_logit = _matmul(hb, w_f, jnp.ones((1, LANES), jnp.float32), jnp.float32, 1024, LANES, "fox_gate_proj")
    b_pad = jnp.pad(b_forget, (0, LANES - FOX_HEADS)).reshape(1, LANES)
    c, ct = _forget_cumsum(f_logit, b_pad, batch, seq)
    return _fox_attention(qkv, c, ct, batch, seq)


def _rotary_tables(seq):
    half = ROT_DIM // 2
    inv_freq = ROPE_THETA ** (-jnp.arange(0, ROT_DIM, 2, dtype=jnp.float32) / ROT_DIM)
    ang = jnp.arange(seq, dtype=jnp.float32)[:, None] * inv_freq[None, :]
    cos, sin = jnp.cos(ang), jnp.sin(ang)
    zeros = jnp.zeros((seq, HEAD_DIM - ROT_DIM), jnp.float32)
    zh = jnp.zeros((seq, half), jnp.float32)
    cos_t = jnp.concatenate([cos, cos, jnp.ones_like(zeros)], axis=1)
    sin_a = jnp.concatenate([zh, sin, zeros], axis=1)
    sin_b = jnp.concatenate([-sin, zh, zeros], axis=1)
    return cos_t, sin_a, sin_b


def _diff_mixer(hb, layer_idx, w_in, lambda_qk, subln_g, tables, batch, seq):
    width = 3 * D_MODEL
    qkv = _matmul(hb, w_in.astype(jnp.bfloat16), _q_scale(width, D_MODEL), jnp.bfloat16, 1024, 512,
                  "diff_in_proj")
    lam_init = 0.8 - 0.6 * math.exp(-0.3 * layer_idx)
    lq = lambda_qk.astype(jnp.float32)
    lam = (jnp.exp(jnp.sum(lq[0] * lq[1])) - jnp.exp(jnp.sum(lq[2] * lq[3])) + lam_init).reshape(1)
    return _diff_attention(qkv, lam, *tables, subln_g.reshape(1, 2 * HEAD_DIM), lam_init, batch, seq)


def kernel(x, fox_w_in_0, fox_b_forget_0, fox_w_out_0, ln_mix_g_0, ln_mix_b_0, moe_w_router_0, moe_b_router_0, moe_w_gate_up_0, moe_w_down_0, moe_ws_gate_up_0, moe_ws_down_0, ln_ffn_g_0, ln_ffn_b_0, diff_w_in_1, diff_lambda_qk_1, diff_subln_g_1, diff_w_out_1, ln_mix_g_1, ln_mix_b_1, moe_w_router_1, moe_b_router_1, moe_w_gate_up_1, moe_w_down_1, moe_ws_gate_up_1, moe_ws_down_1, ln_ffn_g_1, ln_ffn_b_1, fox_w_in_2, fox_b_forget_2, fox_w_out_2, ln_mix_g_2, ln_mix_b_2, moe_w_router_2, moe_b_router_2, moe_w_gate_up_2, moe_w_down_2, moe_ws_gate_up_2, moe_ws_down_2, ln_ffn_g_2, ln_ffn_b_2, diff_w_in_3, diff_lambda_qk_3, diff_subln_g_3, diff_w_out_3, ln_mix_g_3, ln_mix_b_3, moe_w_router_3, moe_b_router_3, moe_w_gate_up_3, moe_w_down_3, moe_ws_gate_up_3, moe_ws_down_3, ln_ffn_g_3, ln_ffn_b_3):
    batch, seq, _ = x.shape
    tokens = batch * seq
    mix = [(fox_w_in_0, fox_b_forget_0, fox_w_out_0),
           (diff_w_in_1, diff_lambda_qk_1, diff_subln_g_1, diff_w_out_1),
           (fox_w_in_2, fox_b_forget_2, fox_w_out_2),
           (diff_w_in_3, diff_lambda_qk_3, diff_subln_g_3, diff_w_out_3)]
    norm_mix = [(ln_mix_g_0, ln_mix_b_0), (ln_mix_g_1, ln_mix_b_1), (ln_mix_g_2, ln_mix_b_2), (ln_mix_g_3, ln_mix_b_3)]
    moe = [(moe_w_router_0, moe_b_router_0, moe_w_gate_up_0, moe_w_down_0, moe_ws_gate_up_0, moe_ws_down_0),
           (moe_w_router_1, moe_b_router_1, moe_w_gate_up_1, moe_w_down_1, moe_ws_gate_up_1, moe_ws_down_1),
           (moe_w_router_2, moe_b_router_2, moe_w_gate_up_2, moe_w_down_2, moe_ws_gate_up_2, moe_ws_down_2),
           (moe_w_router_3, moe_b_router_3, moe_w_gate_up_3, moe_w_down_3, moe_ws_gate_up_3, moe_ws_down_3)]
    norm_ffn = [(ln_ffn_g_0, ln_ffn_b_0), (ln_ffn_g_1, ln_ffn_b_1), (ln_ffn_g_2, ln_ffn_b_2), (ln_ffn_g_3, ln_ffn_b_3)]

    tables = _rotary_tables(seq)
    h = x.reshape(tokens, D_MODEL)
    hb = h.astype(jnp.bfloat16)
    for i in range(DEPTH):
        if i % 2 == 0:
            w_in, b_forget, w_out = mix[i]
            o = _fox_mixer(hb, w_in, b_forget, batch, seq)
        else:
            w_in, lambda_qk, subln_g, w_out = mix[i]
            o = _diff_mixer(hb, i, w_in, lambda_qk, subln_g, tables, batch, seq)
        g, b = norm_mix[i]
        h, hb, h_lin = _proj_ln(o, w_out.astype(jnp.bfloat16), h, g.reshape(1, D_MODEL), b.reshape(1, D_MODEL))
        h, hb = _moe_layer(h, hb, h_lin, *moe[i], *norm_ffn[i])
    return h.reshape(batch, seq, D_MODEL)
```

```python
import functools
import math

import jax
import jax.numpy as jnp
from jax import lax
from jax.experimental import pallas as pl
from jax.experimental.pallas import tpu as pltpu

D_MODEL = 2048
DEPTH = 4
HEAD_DIM = 128
LANES = 128
ROW_CHUNKS = D_MODEL // (2 * LANES)
FOX_HEADS = D_MODEL // HEAD_DIM
DIFF_HEADS = D_MODEL // (2 * HEAD_DIM)
ROT_DIM = HEAD_DIM // 4
ROPE_THETA = 500000.0
N_EXPERTS = 64
TOP_K = 8
N_GROUPS = 8
TOPK_GROUPS = 4
EXPERTS_PER_GROUP = N_EXPERTS // N_GROUPS
EXPERT_DIM = D_MODEL // 4
ROUTED_SCALE = 2.5
DEEPNORM_ALPHA = (2 * DEPTH) ** 0.25
LN_EPS = 1e-5
SUBLN_EPS = 1e-5
LOG2E = math.log2(math.e)

ATT_BLOCK = 256
EXPERT_BLOCK = 256
VMEM_LIMIT = 56 * 1024 * 1024

_HIGHEST = lax.Precision.HIGHEST
_NT = (((1,), (1,)), ((), ()))


def _params(*sem):
    return pltpu.CompilerParams(dimension_semantics=sem, vmem_limit_bytes=VMEM_LIMIT)


def _pack_rows(ref, row0, rows, x):
    half = D_MODEL // 2
    for s in range(ROW_CHUNKS):
        lo = x[:, s * LANES:(s + 1) * LANES].astype(jnp.bfloat16).astype(jnp.float32)
        hi = x[:, half + s * LANES:half + (s + 1) * LANES].astype(jnp.bfloat16).astype(jnp.float32)
        word = lax.bitcast_convert_type(hi, jnp.uint32) | (lax.bitcast_convert_type(lo, jnp.uint32) >> 16)
        ref[pl.ds(row0 * ROW_CHUNKS + s, rows, stride=ROW_CHUNKS), :] = word


def _unpack_chunk(ref, row0, rows, s):
    u = ref[pl.ds(row0 * ROW_CHUNKS + s, rows, stride=ROW_CHUNKS), :]
    lo = lax.bitcast_convert_type(u << 16, jnp.float32)
    hi = lax.bitcast_convert_type(u & jnp.uint32(0xFFFF0000), jnp.float32)
    return lo, hi


def _mm_kernel(a_ref, b_ref, s_ref, o_ref):
    acc = jnp.dot(a_ref[...], b_ref[...], preferred_element_type=jnp.float32)
    o_ref[...] = (acc * s_ref[...]).astype(o_ref.dtype)


def _matmul(a, b, col_scale, out_dtype, tm, tn, name):
    m, k = a.shape
    n = b.shape[1]
    return pl.pallas_call(
        _mm_kernel,
        grid=(m // tm, n // tn),
        in_specs=[pl.BlockSpec((tm, k), lambda i, j: (i, 0)),
                  pl.BlockSpec((k, tn), lambda i, j: (0, j)),
                  pl.BlockSpec((1, tn), lambda i, j: (0, j))],
        out_specs=pl.BlockSpec((tm, tn), lambda i, j: (i, j)),
        out_shape=jax.ShapeDtypeStruct((m, n), out_dtype),
        compiler_params=_params("parallel", "parallel"),
        name=name,
    )(a, b, col_scale)


def _fgate_kernel(fl_ref, b_ref, c_ref, ct_ref, *, seq, chunk):
    x = fl_ref[...] + b_ref[...]
    ls = -(jnp.maximum(-x, 0.0) + jnp.log1p(jnp.exp(-jnp.abs(x))))
    r = lax.broadcasted_iota(jnp.int32, (chunk, chunk), 0)
    c = lax.broadcasted_iota(jnp.int32, (chunk, chunk), 1)
    tri = (r >= c).astype(jnp.float32)
    carry = jnp.zeros((1, LANES), jnp.float32)
    for ch in range(seq // chunk):
        blk = jnp.dot(tri, ls[ch * chunk:(ch + 1) * chunk], precision=_HIGHEST,
                      preferred_element_type=jnp.float32) + carry
        carry = blk[chunk - 1:chunk, :]
        c_ref[0, ch * chunk:(ch + 1) * chunk, :] = blk
        ct_ref[0, :, ch * chunk:(ch + 1) * chunk] = blk.T[:FOX_HEADS]


def _forget_cumsum(f_logit, b_pad, batch, seq):
    chunk = 256
    return pl.pallas_call(
        functools.partial(_fgate_kernel, seq=seq, chunk=chunk),
        grid=(batch,),
        in_specs=[pl.BlockSpec((seq, LANES), lambda b: (b, 0)),
                  pl.BlockSpec((1, LANES), lambda b: (0, 0))],
        out_specs=[pl.BlockSpec((1, seq, LANES), lambda b: (b, 0, 0)),
                   pl.BlockSpec((1, FOX_HEADS, seq), lambda b: (b, 0, 0))],
        out_shape=[jax.ShapeDtypeStruct((batch, seq, LANES), jnp.float32),
                   jax.ShapeDtypeStruct((batch, FOX_HEADS, seq), jnp.float32)],
        compiler_params=_params("parallel"),
        name="forget_cumsum",
    )(f_logit, b_pad)


def _causal_attend(q, k_ref, v_ref, i, row_bias, col_bias):
    blk = ATT_BLOCK
    lo = i * blk
    s_d = lax.dot_general(q, k_ref[lo:lo + blk, :], _NT, preferred_element_type=jnp.float32)
    if col_bias is not None:
        s_d = s_d - col_bias[:, lo:lo + blk]
    r = lax.broadcasted_iota(jnp.int32, (blk, blk), 0)
    c = lax.broadcasted_iota(jnp.int32, (blk, blk), 1)
    s_d = jnp.where(c <= r, s_d, -jnp.inf)
    m = jnp.max(s_d, axis=-1, keepdims=True)
    if i > 0:
        s_o = lax.dot_general(q, k_ref[0:lo, :], _NT, preferred_element_type=jnp.float32)
        if col_bias is not None:
            s_o = s_o - col_bias[:, 0:lo]
        m = jnp.maximum(m, jnp.max(s_o, axis=-1, keepdims=True))
    if row_bias is not None:
        shift = row_bias - (row_bias + m)
    else:
        shift = -m
    p_d = jnp.exp2(s_d + shift)
    l = jnp.sum(p_d, axis=-1, keepdims=True)
    acc = jnp.dot(p_d.astype(jnp.bfloat16), v_ref[lo:lo + blk, :], preferred_element_type=jnp.float32)
    if i > 0:
        p_o = jnp.exp2(s_o + shift)
        l = l + jnp.sum(p_o, axis=-1, keepdims=True)
        acc = acc + jnp.dot(p_o.astype(jnp.bfloat16), v_ref[0:lo, :], preferred_element_type=jnp.float32)
    return acc / l


def _fox_attn_kernel(q_ref, k_ref, v_ref, c_ref, ct_ref, o_ref, *, seq):
    h = pl.program_id(1)
    c_all = c_ref[0]
    lane = lax.broadcasted_iota(jnp.int32, c_all.shape, 1)
    c_row = jnp.sum(jnp.where(lane == h, c_all, 0.0), axis=1, keepdims=True) * LOG2E
    ct_all = ct_ref[0]
    sub = lax.broadcasted_iota(jnp.int32, ct_all.shape, 0)
    c_col = jnp.sum(jnp.where(sub == h, ct_all, 0.0), axis=0, keepdims=True) * LOG2E
    for i in range(seq // ATT_BLOCK):
        lo = i * ATT_BLOCK
        q = q_ref[lo:lo + ATT_BLOCK, :]
        out = _causal_attend(q, k_ref, v_ref, i, c_row[lo:lo + ATT_BLOCK, :], c_col)
        o_ref[lo:lo + ATT_BLOCK, :] = out.astype(o_ref.dtype)


def _fox_attention(qkv, c, ct, batch, seq):
    tokens = batch * seq
    kern = functools.partial(_fox_attn_kernel, seq=seq)
    return pl.pallas_call(
        kern,
        grid=(batch, FOX_HEADS),
        in_specs=[pl.BlockSpec((seq, HEAD_DIM), lambda b, h: (b, h)),
                  pl.BlockSpec((seq, HEAD_DIM), lambda b, h: (b, FOX_HEADS + h)),
                  pl.BlockSpec((seq, HEAD_DIM), lambda b, h: (b, 2 * FOX_HEADS + h)),
                  pl.BlockSpec((1, seq, LANES), lambda b, h: (b, 0, 0)),
                  pl.BlockSpec((1, FOX_HEADS, seq), lambda b, h: (b, 0, 0))],
        out_specs=pl.BlockSpec((seq, HEAD_DIM), lambda b, h: (b, h)),
        out_shape=jax.ShapeDtypeStruct((tokens, D_MODEL), jnp.bfloat16),
        compiler_params=_params("parallel", "parallel"),
        name="fox_attention",
    )(qkv, qkv, qkv, c, ct)


def _diff_attn_kernel(lam_ref, q1_ref, q2_ref, k1_ref, k2_ref, v_ref, cos_ref, sa_ref, sb_ref, g_ref,
                      o_ref, q1s, q2s, k1s, k2s, *, seq, out_scale):
    cos = cos_ref[...]
    sa = sa_ref[...]
    sb = sb_ref[...]
    half = ROT_DIM // 2
    for src, dst in ((q1_ref, q1s), (q2_ref, q2s), (k1_ref, k1s), (k2_ref, k2s)):
        x = src[...].astype(jnp.float32)
        rot = x * cos + pltpu.roll(x, half, axis=1) * sa + pltpu.roll(x, HEAD_DIM - half, axis=1) * sb
        dst[...] = rot.astype(jnp.bfloat16)
    lam = lam_ref[0]
    g = g_ref[...] * out_scale
    for i in range(seq // ATT_BLOCK):
        lo = i * ATT_BLOCK
        a1 = _causal_attend(q1s[lo:lo + ATT_BLOCK, :], k1s, v_ref, i, None, None)
        a2 = _causal_attend(q2s[lo:lo + ATT_BLOCK, :], k2s, v_ref, i, None, None)
        o = a1 - lam * a2
        o = o * lax.rsqrt(jnp.mean(jnp.square(o), axis=-1, keepdims=True) + SUBLN_EPS)
        o_ref[lo:lo + ATT_BLOCK, :] = (o * g).astype(o_ref.dtype)


def _diff_attention(qkv, lam, cos_t, sin_a, sin_b, subln_g, lam_init, batch, seq):
    tokens = batch * seq
    kern = functools.partial(_diff_attn_kernel, seq=seq, out_scale=1.0 - lam_init)
    kv_off = 2 * DIFF_HEADS
    v_off = 2 * DIFF_HEADS
    tab = pl.BlockSpec((seq, HEAD_DIM), lambda b, h: (0, 0))
    return pl.pallas_call(
        kern,
        grid=(batch, DIFF_HEADS),
        in_specs=[pl.BlockSpec(memory_space=pltpu.SMEM),
                  pl.BlockSpec((seq, HEAD_DIM), lambda b, h: (b, 2 * h)),
                  pl.BlockSpec((seq, HEAD_DIM), lambda b, h: (b, 2 * h + 1)),
                  pl.BlockSpec((seq, HEAD_DIM), lambda b, h: (b, kv_off + 2 * h)),
                  pl.BlockSpec((seq, HEAD_DIM), lambda b, h: (b, kv_off + 2 * h + 1)),
                  pl.BlockSpec((seq, 2 * HEAD_DIM), lambda b, h: (b, v_off + h)),
                  tab, tab, tab,
                  pl.BlockSpec((1, 2 * HEAD_DIM), lambda b, h: (0, 0))],
        out_specs=pl.BlockSpec((seq, 2 * HEAD_DIM), lambda b, h: (b, h)),
        out_shape=jax.ShapeDtypeStruct((tokens, D_MODEL), jnp.bfloat16),
        scratch_shapes=[pltpu.VMEM((seq, HEAD_DIM), jnp.bfloat16)] * 4,
        compiler_params=_params("parallel", "parallel"),
        name="diff_attention",
    )(lam, qkv, qkv, qkv, qkv, qkv, cos_t, sin_a, sin_b, subln_g)


def _deepnorm(z, g, b):
    mu = jnp.mean(z, axis=-1, keepdims=True)
    zc = z - mu
    var = jnp.mean(zc * zc, axis=-1, keepdims=True)
    return zc * lax.rsqrt(var + LN_EPS) * g + b


def _proj_ln_kernel(a_ref, w_ref, h_ref, g_ref, b_ref, of_ref, ob_ref, ol_ref, *, tm):
    y = jnp.dot(a_ref[...], w_ref[...], preferred_element_type=jnp.float32)
    out = _deepnorm(DEEPNORM_ALPHA * h_ref[...] + y, g_ref[...], b_ref[...])
    of_ref[...] = out
    ob_ref[...] = out.astype(jnp.bfloat16)
    _pack_rows(ol_ref, 0, tm, out)


def _proj_ln(a, w, h, g, b):
    tokens = a.shape[0]
    tm = 256
    row = pl.BlockSpec((tm, D_MODEL), lambda i: (i, 0))
    vec = pl.BlockSpec((1, D_MODEL), lambda i: (0, 0))
    return pl.pallas_call(
        functools.partial(_proj_ln_kernel, tm=tm),
        grid=(tokens // tm,),
        in_specs=[row, pl.BlockSpec((D_MODEL, D_MODEL), lambda i: (0, 0)), row, vec, vec],
        out_specs=[row, row, pl.BlockSpec((tm * ROW_CHUNKS, LANES), lambda i: (i, 0))],
        out_shape=[jax.ShapeDtypeStruct((tokens, D_MODEL), jnp.float32),
                   jax.ShapeDtypeStruct((tokens, D_MODEL), jnp.bfloat16),
                   jax.ShapeDtypeStruct((tokens * ROW_CHUNKS, LANES), jnp.uint32)],
        compiler_params=_params("parallel"),
        name="outproj_deepnorm",
    )(a, w, h, g, b)


def _router_kernel(h_ref, wt_ref, bias_ref, idx_ref, w_ref, rank_ref, cnt_ref, carry_ref, *, tm):
    step = pl.program_id(0)

    @pl.when(step == 0)
    def _():
        carry_ref[...] = jnp.zeros_like(carry_ref)

    logits = lax.dot_general(wt_ref[...], h_ref[...], _NT, precision=_HIGHEST,
                             preferred_element_type=jnp.float32)
    scores = 1.0 / (1.0 + jnp.exp(-logits))
    biased = scores + bias_ref[...]
    npg = EXPERTS_PER_GROUP
    group = lax.broadcasted_iota(jnp.int32, (N_GROUPS, tm), 0)
    s = [scores[j * N_GROUPS:(j + 1) * N_GROUPS] for j in range(npg)]
    bz = [biased[j * N_GROUPS:(j + 1) * N_GROUPS] for j in range(npg)]
    eid = [group * npg + j for j in range(npg)]
    neg = -jnp.inf

    m1 = functools.reduce(jnp.maximum, bz)
    j1 = functools.reduce(jnp.minimum, [jnp.where(bz[j] == m1, j, npg) for j in range(npg)])
    m2 = functools.reduce(jnp.maximum, [jnp.where(j1 == j, neg, bz[j]) for j in range(npg)])
    gs = m1 + m2
    beaten = jnp.zeros((N_GROUPS, tm), jnp.int32)
    for r in range(1, N_GROUPS):
        og = pltpu.roll(gs, r, axis=0)
        oi = pltpu.roll(group, r, axis=0)
        wins = jnp.where(og > gs, 1, jnp.where(og == gs, jnp.where(oi < group, 1, 0), 0))
        beaten = beaten + wins
    keep = beaten < TOPK_GROUPS
    mb = [jnp.where(keep, bz[j], neg) for j in range(npg)]

    sel_idx, sel_w = [], []
    chosen = [jnp.zeros((N_GROUPS, tm), jnp.float32) for _ in range(npg)]
    for _ in range(TOP_K):
        m = jnp.max(functools.reduce(jnp.maximum, mb), axis=0, keepdims=True)
        cand = functools.reduce(jnp.minimum, [jnp.where(mb[j] == m, eid[j], N_EXPERTS) for j in range(npg)])
        ik = jnp.min(cand, axis=0, keepdims=True)
        hit = [eid[j] == ik for j in range(npg)]
        wk = functools.reduce(jnp.add, [jnp.where(hit[j], s[j], 0.0) for j in range(npg)])
        sel_idx.append(ik)
        sel_w.append(jnp.sum(wk, axis=0, keepdims=True))
        mb = [jnp.where(hit[j], neg, mb[j]) for j in range(npg)]
        chosen = [jnp.where(hit[j], 1.0, chosen[j]) for j in range(npg)]
    wsum = functools.reduce(jnp.add, sel_w)
    idx_ref[...] = jnp.concatenate(sel_idx, axis=0)
    w_ref[...] = jnp.concatenate(sel_w, axis=0) / wsum * ROUTED_SCALE

    onehot = jnp.concatenate(chosen, axis=0)
    tr = lax.broadcasted_iota(jnp.int32, (tm, tm), 0)
    tc = lax.broadcasted_iota(jnp.int32, (tm, tm), 1)
    before = (tr < tc).astype(jnp.bfloat16)
    excl = jnp.dot(onehot.astype(jnp.bfloat16), before, preferred_element_type=jnp.float32)
    base = excl + carry_ref[:, 0:1]
    bs = [base[j * N_GROUPS:(j + 1) * N_GROUPS] for j in range(npg)]
    ranks = []
    for k in range(TOP_K):
        rk = functools.reduce(jnp.add, [jnp.where(eid[j] == sel_idx[k], bs[j], 0.0) for j in range(npg)])
        ranks.append(jnp.sum(rk, axis=0, keepdims=True))
    rank_ref[...] = jnp.concatenate(ranks, axis=0).astype(jnp.int32)
    carry_ref[...] = carry_ref[...] + jnp.sum(onehot, axis=1, keepdims=True)
    cnt_ref[...] = carry_ref[...]


def _router(h, wt_perm, bias_perm):
    tokens = h.shape[0]
    tm = 512
    sel = pl.BlockSpec((TOP_K, tm), lambda i: (0, i))
    return pl.pallas_call(
        functools.partial(_router_kernel, tm=tm),
        grid=(tokens // tm,),
        in_specs=[pl.BlockSpec((tm, D_MODEL), lambda i: (i, 0)),
                  pl.BlockSpec((N_EXPERTS, D_MODEL), lambda i: (0, 0)),
                  pl.BlockSpec((N_EXPERTS, 1), lambda i: (0, 0))],
        out_specs=[sel, sel, sel, pl.BlockSpec((N_EXPERTS, LANES), lambda i: (0, 0))],
        out_shape=[jax.ShapeDtypeStruct((TOP_K, tokens), jnp.int32),
                   jax.ShapeDtypeStruct((TOP_K, tokens), jnp.float32),
                   jax.ShapeDtypeStruct((TOP_K, tokens), jnp.int32),
                   jax.ShapeDtypeStruct((N_EXPERTS, LANES), jnp.float32)],
        scratch_shapes=[pltpu.VMEM((N_EXPERTS, LANES), jnp.float32)],
        compiler_params=_params("arbitrary"),
        name="moe_router",
    )(h, wt_perm, bias_perm)


def _dispatch_kernel(lo_ref, hi_ref, pos_ref, h3_ref, xs_ref, zbuf, sem, zsem, *, tt):
    step = pl.program_id(0)

    def issue(t, carry):
        for k in range(TOP_K):
            pltpu.make_async_copy(h3_ref.at[t], xs_ref.at[pos_ref[k, t]], sem).start(priority=k % 2)
        return carry

    lax.fori_loop(0, tt, issue, 0)

    @pl.when(step == 0)
    def _():
        zbuf[...] = jnp.zeros_like(zbuf)

        def per_range(e, carry):
            def fill(r, c2):
                pltpu.make_async_copy(zbuf, xs_ref.at[r], zsem).start()
                return c2

            def drain(r, c2):
                pltpu.make_async_copy(zbuf, xs_ref.at[0], zsem).wait()
                return c2

            lax.fori_loop(lo_ref[e], hi_ref[e], fill, 0)
            lax.fori_loop(lo_ref[e], hi_ref[e], drain, 0)
            return carry

        lax.fori_loop(0, N_EXPERTS + 1, per_range, 0)

    rows = pl.ds(0, tt * TOP_K)
    pltpu.make_async_copy(xs_ref.at[rows], xs_ref.at[rows], sem).wait()


def _dispatch(pad_lo, pad_hi, pos, h_lin, n_rows):
    tokens = h_lin.shape[0] // ROW_CHUNKS
    tt = 256
    h3 = h_lin.reshape(tokens, ROW_CHUNKS, LANES)
    xs = pl.pallas_call(
        functools.partial(_dispatch_kernel, tt=tt),
        grid_spec=pltpu.PrefetchScalarGridSpec(
            num_scalar_prefetch=2,
            grid=(tokens // tt,),
            in_specs=[pl.BlockSpec((TOP_K, tt), lambda i, lo, hi: (0, i), memory_space=pltpu.SMEM),
                      pl.BlockSpec((tt, ROW_CHUNKS, LANES), lambda i, lo, hi: (i, 0, 0))],
            out_specs=pl.BlockSpec(memory_space=pl.ANY),
            scratch_shapes=[pltpu.VMEM((ROW_CHUNKS, LANES), jnp.uint32),
                            pltpu.SemaphoreType.DMA, pltpu.SemaphoreType.DMA]),
        out_shape=jax.ShapeDtypeStruct((n_rows, ROW_CHUNKS, LANES), jnp.uint32),
        compiler_params=_params("arbitrary"),
        name="moe_dispatch",
    )(pad_lo, pad_hi, pos, h3)
    return xs.reshape(n_rows * ROW_CHUNKS, LANES)


def _silu_mul(gate_up):
    half = gate_up.shape[-1] // 2
    gate = gate_up[:, :half]
    up = gate_up[:, half:]
    return (gate / (1.0 + jnp.exp(-gate)) * up).astype(jnp.bfloat16)


def _expert_kernel(be_ref, nxt_ref, nu_ref, x_ref, wgu_hbm, wd_hbm, y_ref, wgu_f, wd_f, wgu_b, wd_b,
                   sem_gu, sem_d):
    b = pl.program_id(0)
    active = b < nu_ref[0]
    expert = be_ref[b]
    prev = be_ref[jnp.maximum(b - 1, 0)]

    def weight_copies(e):
        return (pltpu.make_async_copy(wgu_hbm.at[e], wgu_f, sem_gu),
                pltpu.make_async_copy(wd_hbm.at[e], wd_f, sem_d))

    @pl.when(b == 0)
    def _():
        for copy in weight_copies(expert):
            copy.start()

    @pl.when(active & ((b == 0) | (expert != prev)))
    def _():
        for copy in weight_copies(expert):
            copy.wait()
        wgu_b[...] = wgu_f[...].astype(jnp.bfloat16)
        wd_b[...] = wd_f[...].astype(jnp.bfloat16)
        following = nxt_ref[b]

        @pl.when(following != expert)
        def _():
            for copy in weight_copies(following):
                copy.start()

    @pl.when(active)
    def _():
        chunks = [_unpack_chunk(x_ref, 0, EXPERT_BLOCK, s) for s in range(ROW_CHUNKS)]
        x = jnp.concatenate([lo for lo, _ in chunks] + [hi for _, hi in chunks],
                            axis=-1).astype(jnp.bfloat16)
        act = _silu_mul(jnp.dot(x, wgu_b[...], preferred_element_type=jnp.float32))
        y = jnp.dot(act, wd_b[...], preferred_element_type=jnp.float32)
        _pack_rows(y_ref, 0, EXPERT_BLOCK, y)

    @pl.when(jnp.logical_not(active))
    def _():
        y_ref[...] = jnp.zeros_like(y_ref)


def _experts(block_expert, block_next, n_used, x_lin, wgu, wd):
    n_blocks = block_expert.shape[0]
    rows = EXPERT_BLOCK * ROW_CHUNKS
    return pl.pallas_call(
        _expert_kernel,
        grid_spec=pltpu.PrefetchScalarGridSpec(
            num_scalar_prefetch=3,
            grid=(n_blocks,),
            in_specs=[pl.BlockSpec((rows, LANES), lambda b, be, nx, nu: (b, 0)),
                      pl.BlockSpec(memory_space=pl.ANY),
                      pl.BlockSpec(memory_space=pl.ANY)],
            out_specs=pl.BlockSpec((rows, LANES), lambda b, be, nx, nu: (b, 0)),
            scratch_shapes=[pltpu.VMEM((D_MODEL, 2 * EXPERT_DIM), jnp.float32),
                            pltpu.VMEM((EXPERT_DIM, D_MODEL), jnp.float32),
                            pltpu.VMEM((D_MODEL, 2 * EXPERT_DIM), jnp.bfloat16),
                            pltpu.VMEM((EXPERT_DIM, D_MODEL), jnp.bfloat16),
                            pltpu.SemaphoreType.DMA, pltpu.SemaphoreType.DMA]),
        out_shape=jax.ShapeDtypeStruct(x_lin.shape, jnp.uint32),
        compiler_params=_params("arbitrary"),
        name="moe_experts",
    )(block_expert, block_next, n_used, x_lin, wgu, wd)


def _moe_out_kernel(pos_ref, posn_ref, hb_ref, h_ref, ys_ref, w_ref, wsgu_ref, wsd_ref, g_ref, b_ref,
                    of_ref, ob_ref, buf0, buf1, sem0, sem1, *, tm):
    step = pl.program_id(0)
    nstep = pl.num_programs(0)

    def start_row(p_ref, col, t, k, buf, sem):
        dst = (k * tm + t) * ROW_CHUNKS
        if not isinstance(dst, int):
            dst = pl.multiple_of(dst, ROW_CHUNKS)
        pltpu.make_async_copy(ys_ref.at[p_ref[k, col]], buf.at[pl.ds(dst, ROW_CHUNKS), :], sem).start()

    def gather(p_ref, off, buf, sem):
        for t in range(tm):
            for k in range(TOP_K):
                start_row(p_ref, off + t, t, k, buf, sem)

    def gather_rolled(p_ref, off, buf, sem):
        def body(t, carry):
            for k in range(TOP_K):
                start_row(p_ref, off + t, t, k, buf, sem)
            return carry

        lax.fori_loop(0, tm, body, 0)

    def wait_all(buf, sem):
        pltpu.make_async_copy(buf, buf, sem).wait()

    def compute(buf, r0):
        rows = slice(r0, r0 + tm)
        act = _silu_mul(jnp.dot(hb_ref[rows, :], wsgu_ref[...], preferred_element_type=jnp.float32))
        y = jnp.dot(act, wsd_ref[...], preferred_element_type=jnp.float32)
        w = w_ref[rows, :]
        wk = [jnp.broadcast_to(w[:, k:k + 1], (tm, LANES)) for k in range(TOP_K)]
        acc_lo, acc_hi = [], []
        for s in range(ROW_CHUNKS):
            a_lo = jnp.zeros((tm, LANES), jnp.float32)
            a_hi = jnp.zeros((tm, LANES), jnp.float32)
            for k in range(TOP_K):
                lo, hi = _unpack_chunk(buf, k * tm, tm, s)
                a_lo = a_lo + wk[k] * lo
                a_hi = a_hi + wk[k] * hi
            acc_lo.append(a_lo)
            acc_hi.append(a_hi)
        y = y + jnp.concatenate(acc_lo + acc_hi, axis=-1)
        out = _deepnorm(DEEPNORM_ALPHA * h_ref[rows, :] + y, g_ref[...], b_ref[...])
        of_ref[rows, :] = out
        ob_ref[rows, :] = out.astype(jnp.bfloat16)

    @pl.when(step == 0)
    def _():
        gather_rolled(pos_ref, 0, buf0, sem0)

    gather(pos_ref, tm, buf1, sem1)
    wait_all(buf0, sem0)
    compute(buf0, 0)
    gather(posn_ref, 0, buf0, sem0)
    wait_all(buf1, sem1)
    compute(buf1, tm)

    @pl.when(step == nstep - 1)
    def _():
        wait_all(buf0, sem0)


def _moe_out(pos, hb, h, y_lin, w_tk, wsgu, wsd, g, b):
    tokens = h.shape[0]
    tm = 128
    nstep = tokens // (2 * tm)
    n_rows = y_lin.shape[0] // ROW_CHUNKS
    row = pl.BlockSpec((2 * tm, D_MODEL), lambda i: (i, 0))
    vec = pl.BlockSpec((1, D_MODEL), lambda i: (0, 0))
    return pl.pallas_call(
        functools.partial(_moe_out_kernel, tm=tm),
        grid=(nstep,),
        in_specs=[pl.BlockSpec((TOP_K, 2 * tm), lambda i: (0, i), memory_space=pltpu.SMEM),
                  pl.BlockSpec((TOP_K, 2 * tm), lambda i: (0, jnp.minimum(i + 1, nstep - 1)),
                               memory_space=pltpu.SMEM),
                  row, row,
                  pl.BlockSpec(memory_space=pl.ANY),
                  pl.BlockSpec((2 * tm, TOP_K), lambda i: (i, 0)),
                  pl.BlockSpec((D_MODEL, 2 * EXPERT_DIM), lambda i: (0, 0)),
                  pl.BlockSpec((EXPERT_DIM, D_MODEL), lambda i: (0, 0)),
                  vec, vec],
        out_specs=[row, row],
        out_shape=[jax.ShapeDtypeStruct((tokens, D_MODEL), jnp.float32),
                   jax.ShapeDtypeStruct((tokens, D_MODEL), jnp.bfloat16)],
        scratch_shapes=[pltpu.VMEM((TOP_K * tm * ROW_CHUNKS, LANES), jnp.uint32),
                        pltpu.VMEM((TOP_K * tm * ROW_CHUNKS, LANES), jnp.uint32),
                        pltpu.SemaphoreType.DMA, pltpu.SemaphoreType.DMA],
        compiler_params=_params("arbitrary"),
        name="moe_combine_deepnorm",
    )(pos, pos, hb, h, y_lin.reshape(n_rows, ROW_CHUNKS, LANES), w_tk, wsgu, wsd, g, b)


def _moe_layer(h, hb, h_lin, w_router, b_router, w_gate_up, w_down, ws_gate_up, ws_down, g, b):
    tokens = h.shape[0]
    n_assign = tokens * TOP_K
    n_blocks = n_assign // EXPERT_BLOCK + N_EXPERTS
    n_rows = n_blocks * EXPERT_BLOCK
    perm = (jnp.arange(N_EXPERTS) % N_GROUPS) * EXPERTS_PER_GROUP + jnp.arange(N_EXPERTS) // N_GROUPS
    top_idx, top_w, rank, cnt_perm = _router(h, w_router.T[perm], b_router[perm].reshape(N_EXPERTS, 1))

    counts = jnp.zeros((N_EXPERTS,), jnp.int32).at[perm].set(cnt_perm[:, 0].astype(jnp.int32))
    padded = (counts + EXPERT_BLOCK - 1) // EXPERT_BLOCK * EXPERT_BLOCK
    padded_end = jnp.cumsum(padded)
    padded_start = padded_end - padded
    experts = jnp.arange(N_EXPERTS, dtype=jnp.int32)[:, None, None]
    pos = rank + jnp.sum(jnp.where(top_idx[None] == experts, padded_start[:, None, None], 0), axis=0)
    block_row = jnp.arange(n_blocks, dtype=jnp.int32) * EXPERT_BLOCK
    block_expert = jnp.minimum(jnp.sum((padded_end[None, :] <= block_row[:, None]).astype(jnp.int32), axis=1),
                               N_EXPERTS - 1)
    n_used = (padded_end[-1:] // EXPERT_BLOCK).astype(jnp.int32)
    ids = jnp.arange(N_EXPERTS, dtype=jnp.int32)
    later = jnp.where((counts > 0)[None, :] & (ids[None, :] > ids[:, None]), ids[None, :], N_EXPERTS)
    following = jnp.min(later, axis=1)
    expert_next = jnp.where(following < N_EXPERTS, following, ids)
    block_next = jnp.sum(jnp.where(block_expert[:, None] == ids[None, :], expert_next[None, :], 0), axis=1)
    pad_lo = jnp.concatenate([padded_start + counts, padded_end[-1:]]).astype(jnp.int32)
    pad_hi = jnp.concatenate([padded_end, jnp.full((1,), n_rows)]).astype(jnp.int32)

    x_lin = _dispatch(pad_lo, pad_hi, pos, h_lin, n_rows)
    y_lin = _experts(block_expert, block_next.astype(jnp.int32), n_used, x_lin, w_gate_up, w_down)
    return _moe_out(pos, hb, h, y_lin, top_w.T, ws_gate_up.astype(jnp.bfloat16), ws_down.astype(jnp.bfloat16),
                    g.reshape(1, D_MODEL), b.reshape(1, D_MODEL))


def _q_scale(width, q_cols):
    return jnp.where(jnp.arange(width) < q_cols, HEAD_DIM ** -0.5 * LOG2E, 1.0).astype(jnp.float32).reshape(1, width)


def _fox_mixer(hb, w_in, b_forget, batch, seq):
    width = 3 * D_MODEL
    qkv = _matmul(hb, w_in[:, :width].astype(jnp.bfloat16), _q_scale(width, D_MODEL), jnp.bfloat16,
                  1024, 512, "fox_in_proj")
    w_f = jnp.pad(w_in[:, width:], ((0, 0), (0, LANES - FOX_HEADS))).astype(jnp.bfloat16)
    f_logit = _matmul(hb, w_f, jnp.ones((1, LANES), jnp.float32), jnp.float32, 1024, LANES, "fox_gate_proj")
    b_pad = jnp.pad(b_forget, (0, LANES - FOX_HEADS)).reshape(1, LANES)
    c, ct = _forget_cumsum(f_logit, b_pad, batch, seq)
    return _fox_attention(qkv, c, ct, batch, seq)


def _rotary_tables(seq):
    half = ROT_DIM // 2
    inv_freq = ROPE_THETA ** (-jnp.arange(0, ROT_DIM, 2, dtype=jnp.float32) / ROT_DIM)
    ang = jnp.arange(seq, dtype=jnp.float32)[:, None] * inv_freq[None, :]
    cos, sin = jnp.cos(ang), jnp.sin(ang)
    zeros = jnp.zeros((seq, HEAD_DIM - ROT_DIM), jnp.float32)
    zh = jnp.zeros((seq, half), jnp.float32)
    cos_t = jnp.concatenate([cos, cos, jnp.ones_like(zeros)], axis=1)
    sin_a = jnp.concatenate([zh, sin, zeros], axis=1)
    sin_b = jnp.concatenate([-sin, zh, zeros], axis=1)
    return cos_t, sin_a, sin_b


def _diff_mixer(hb, layer_idx, w_in, lambda_qk, subln_g, tables, batch, seq):
    width = 3 * D_MODEL
    qkv = _matmul(hb, w_in.astype(jnp.bfloat16), _q_scale(width, D_MODEL), jnp.bfloat16, 1024, 512,
                  "diff_in_proj")
    lam_init = 0.8 - 0.6 * math.exp(-0.3 * layer_idx)
    lq = lambda_qk.astype(jnp.float32)
    lam = (jnp.exp(jnp.sum(lq[0] * lq[1])) - jnp.exp(jnp.sum(lq[2] * lq[3])) + lam_init).reshape(1)
    return _diff_attention(qkv, lam, *tables, subln_g.reshape(1, 2 * HEAD_DIM), lam_init, batch, seq)


def kernel(x, fox_w_in_0, fox_b_forget_0, fox_w_out_0, ln_mix_g_0, ln_mix_b_0, moe_w_router_0, moe_b_router_0, moe_w_gate_up_0, moe_w_down_0, moe_ws_gate_up_0, moe_ws_down_0, ln_ffn_g_0, ln_ffn_b_0, diff_w_in_1, diff_lambda_qk_1, diff_subln_g_1, diff_w_out_1, ln_mix_g_1, ln_mix_b_1, moe_w_router_1, moe_b_router_1, moe_w_gate_up_1, moe_w_down_1, moe_ws_gate_up_1, moe_ws_down_1, ln_ffn_g_1, ln_ffn_b_1, fox_w_in_2, fox_b_forget_2, fox_w_out_2, ln_mix_g_2, ln_mix_b_2, moe_w_router_2, moe_b_router_2, moe_w_gate_up_2, moe_w_down_2, moe_ws_gate_up_2, moe_ws_down_2, ln_ffn_g_2, ln_ffn_b_2, diff_w_in_3, diff_lambda_qk_3, diff_subln_g_3, diff_w_out_3, ln_mix_g_3, ln_mix_b_3, moe_w_router_3, moe_b_router_3, moe_w_gate_up_3, moe_w_down_3, moe_ws_gate_up_3, moe_ws_down_3, ln_ffn_g_3, ln_ffn_b_3):
    batch, seq, _ = x.shape
    tokens = batch * seq
    mix = [(fox_w_in_0, fox_b_forget_0, fox_w_out_0),
           (diff_w_in_1, diff_lambda_qk_1, diff_subln_g_1, diff_w_out_1),
           (fox_w_in_2, fox_b_forget_2, fox_w_out_2),
           (diff_w_in_3, diff_lambda_qk_3, diff_subln_g_3, diff_w_out_3)]
    norm_mix = [(ln_mix_g_0, ln_mix_b_0), (ln_mix_g_1, ln_mix_b_1), (ln_mix_g_2, ln_mix_b_2), (ln_mix_g_3, ln_mix_b_3)]
    moe = [(moe_w_router_0, moe_b_router_0, moe_w_gate_up_0, moe_w_down_0, moe_ws_gate_up_0, moe_ws_down_0),
           (moe_w_router_1, moe_b_router_1, moe_w_gate_up_1, moe_w_down_1, moe_ws_gate_up_1, moe_ws_down_1),
           (moe_w_router_2, moe_b_router_2, moe_w_gate_up_2, moe_w_down_2, moe_ws_gate_up_2, moe_ws_down_2),
           (moe_w_router_3, moe_b_router_3, moe_w_gate_up_3, moe_w_down_3, moe_ws_gate_up_3, moe_ws_down_3)]
    norm_ffn = [(ln_ffn_g_0, ln_ffn_b_0), (ln_ffn_g_1, ln_ffn_b_1), (ln_ffn_g_2, ln_ffn_b_2), (ln_ffn_g_3, ln_ffn_b_3)]

    tables = _rotary_tables(seq)
    h = x.reshape(tokens, D_MODEL)
    hb = h.astype(jnp.bfloat16)
    for i in range(DEPTH):
        if i % 2 == 0:
            w_in, b_forget, w_out = mix[i]
            o = _fox_mixer(hb, w_in, b_forget, batch, seq)
        else:
            w_in, lambda_qk, subln_g, w_out = mix[i]
            o = _diff_mixer(hb, i, w_in, lambda_qk, subln_g, tables, batch, seq)
        g, b = norm_mix[i]
        h, hb, h_lin = _proj_ln(o, w_out.astype(jnp.bfloat16), h, g.reshape(1, D_MODEL), b.reshape(1, D_MODEL))
        h, hb = _moe_layer(h, hb, h_lin, *moe[i], *norm_ffn[i])
    return h.reshape(batch, seq, D_MODEL)
```

```python
import functools
import math

import jax
import jax.numpy as jnp
from jax import lax
from jax.experimental import pallas as pl
from jax.experimental.pallas import tpu as pltpu

D_MODEL = 2048
DEPTH = 4
HEAD_DIM = 128
LANES = 128
ROW_CHUNKS = D_MODEL // (2 * LANES)
FOX_HEADS = D_MODEL // HEAD_DIM
DIFF_HEADS = D_MODEL // (2 * HEAD_DIM)
ROT_DIM = HEAD_DIM // 4
ROPE_THETA = 500000.0
N_EXPERTS = 64
TOP_K = 8
N_GROUPS = 8
TOPK_GROUPS = 4
EXPERTS_PER_GROUP = N_EXPERTS // N_GROUPS
EXPERT_DIM = D_MODEL // 4
ROUTED_SCALE = 2.5
DEEPNORM_ALPHA = (2 * DEPTH) ** 0.25
LN_EPS = 1e-5
SUBLN_EPS = 1e-5
LOG2E = math.log2(math.e)

ATT_BLOCK = 256
EXPERT_BLOCK = 512
VMEM_LIMIT = 56 * 1024 * 1024

_HIGHEST = lax.Precision.HIGHEST
_NT = (((1,), (1,)), ((), ()))


def _params(*sem):
    return pltpu.CompilerParams(dimension_semantics=sem, vmem_limit_bytes=VMEM_LIMIT)


def _pack_rows(ref, row0, rows, x):
    half = D_MODEL // 2
    for s in range(ROW_CHUNKS):
        lo = x[:, s * LANES:(s + 1) * LANES].astype(jnp.bfloat16).astype(jnp.float32)
        hi = x[:, half + s * LANES:half + (s + 1) * LANES].astype(jnp.bfloat16).astype(jnp.float32)
        word = lax.bitcast_convert_type(hi, jnp.uint32) | (lax.bitcast_convert_type(lo, jnp.uint32) >> 16)
        ref[pl.ds(row0 * ROW_CHUNKS + s, rows, stride=ROW_CHUNKS), :] = word


def _unpack_chunk(ref, row0, rows, s):
    u = ref[pl.ds(row0 * ROW_CHUNKS + s, rows, stride=ROW_CHUNKS), :]
    lo = lax.bitcast_convert_type(u << 16, jnp.float32)
    hi = lax.bitcast_convert_type(u & jnp.uint32(0xFFFF0000), jnp.float32)
    return lo, hi


def _mm_kernel(a_ref, b_ref, s_ref, o_ref):
    acc = jnp.dot(a_ref[...], b_ref[...], preferred_element_type=jnp.float32)
    o_ref[...] = (acc * s_ref[...]).astype(o_ref.dtype)


def _matmul(a, b, col_scale, out_dtype, tm, tn, name):
    m, k = a.shape
    n = b.shape[1]
    return pl.pallas_call(
        _mm_kernel,
        grid=(m // tm, n // tn),
        in_specs=[pl.BlockSpec((tm, k), lambda i, j: (i, 0)),
                  pl.BlockSpec((k, tn), lambda i, j: (0, j)),
                  pl.BlockSpec((1, tn), lambda i, j: (0, j))],
        out_specs=pl.BlockSpec((tm, tn), lambda i, j: (i, j)),
        out_shape=jax.ShapeDtypeStruct((m, n), out_dtype),
        compiler_params=_params("parallel", "parallel"),
        name=name,
    )(a, b, col_scale)


def _fgate_kernel(fl_ref, b_ref, c_ref, ct_ref, *, seq, chunk):
    x = fl_ref[...] + b_ref[...]
    ls = -(jnp.maximum(-x, 0.0) + jnp.log1p(jnp.exp(-jnp.abs(x))))
    r = lax.broadcasted_iota(jnp.int32, (chunk, chunk), 0)
    c = lax.broadcasted_iota(jnp.int32, (chunk, chunk), 1)
    tri = (r >= c).astype(jnp.float32)
    carry = jnp.zeros((1, LANES), jnp.float32)
    for ch in range(seq // chunk):
        blk = jnp.dot(tri, ls[ch * chunk:(ch + 1) * chunk], precision=_HIGHEST,
                      preferred_element_type=jnp.float32) + carry
        carry = blk[chunk - 1:chunk, :]
        c_ref[0, ch * chunk:(ch + 1) * chunk, :] = blk
        ct_ref[0, :, ch * chunk:(ch + 1) * chunk] = blk.T[:FOX_HEADS]


def _forget_cumsum(f_logit, b_pad, batch, seq):
    chunk = 256
    return pl.pallas_call(
        functools.partial(_fgate_kernel, seq=seq, chunk=chunk),
        grid=(batch,),
        in_specs=[pl.BlockSpec((seq, LANES), lambda b: (b, 0)),
                  pl.BlockSpec((1, LANES), lambda b: (0, 0))],
        out_specs=[pl.BlockSpec((1, seq, LANES), lambda b: (b, 0, 0)),
                   pl.BlockSpec((1, FOX_HEADS, seq), lambda b: (b, 0, 0))],
        out_shape=[jax.ShapeDtypeStruct((batch, seq, LANES), jnp.float32),
                   jax.ShapeDtypeStruct((batch, FOX_HEADS, seq), jnp.float32)],
        compiler_params=_params("parallel"),
        name="forget_cumsum",
    )(f_logit, b_pad)


def _causal_attend(q, k_ref, v_ref, i, row_bias, col_bias):
    blk = ATT_BLOCK
    lo = i * blk
    s_d = lax.dot_general(q, k_ref[lo:lo + blk, :], _NT, preferred_element_type=jnp.float32)
    if col_bias is not None:
        s_d = s_d - col_bias[:, lo:lo + blk]
    r = lax.broadcasted_iota(jnp.int32, (blk, blk), 0)
    c = lax.broadcasted_iota(jnp.int32, (blk, blk), 1)
    s_d = jnp.where(c <= r, s_d, -jnp.inf)
    m = jnp.max(s_d, axis=-1, keepdims=True)
    if i > 0:
        s_o = lax.dot_general(q, k_ref[0:lo, :], _NT, preferred_element_type=jnp.float32)
        if col_bias is not None:
            s_o = s_o - col_bias[:, 0:lo]
        m = jnp.maximum(m, jnp.max(s_o, axis=-1, keepdims=True))
    if row_bias is not None:
        shift = row_bias - (row_bias + m)
    else:
        shift = -m
    p_d = jnp.exp2(s_d + shift)
    l = jnp.sum(p_d, axis=-1, keepdims=True)
    acc = jnp.dot(p_d.astype(jnp.bfloat16), v_ref[lo:lo + blk, :], preferred_element_type=jnp.float32)
    if i > 0:
        p_o = jnp.exp2(s_o + shift)
        l = l + jnp.sum(p_o, axis=-1, keepdims=True)
        acc = acc + jnp.dot(p_o.astype(jnp.bfloat16), v_ref[0:lo, :], preferred_element_type=jnp.float32)
    return acc / l


def _fox_attn_kernel(q_ref, k_ref, v_ref, c_ref, ct_ref, o_ref, *, seq):
    h = pl.program_id(1)
    c_all = c_ref[0]
    lane = lax.broadcasted_iota(jnp.int32, c_all.shape, 1)
    c_row = jnp.sum(jnp.where(lane == h, c_all, 0.0), axis=1, keepdims=True) * LOG2E
    ct_all = ct_ref[0]
    sub = lax.broadcasted_iota(jnp.int32, ct_all.shape, 0)
    c_col = jnp.sum(jnp.where(sub == h, ct_all, 0.0), axis=0, keepdims=True) * LOG2E
    for i in range(seq // ATT_BLOCK):
        lo = i * ATT_BLOCK
        q = q_ref[lo:lo + ATT_BLOCK, :]
        out = _causal_attend(q, k_ref, v_ref, i, c_row[lo:lo + ATT_BLOCK, :], c_col)
        o_ref[lo:lo + ATT_BLOCK, :] = out.astype(o_ref.dtype)


def _fox_attention(qkv, c, ct, batch, seq):
    tokens = batch * seq
    kern = functools.partial(_fox_attn_kernel, seq=seq)
    return pl.pallas_call(
        kern,
        grid=(batch, FOX_HEADS),
        in_specs=[pl.BlockSpec((seq, HEAD_DIM), lambda b, h: (b, h)),
                  pl.BlockSpec((seq, HEAD_DIM), lambda b, h: (b, FOX_HEADS + h)),
                  pl.BlockSpec((seq, HEAD_DIM), lambda b, h: (b, 2 * FOX_HEADS + h)),
                  pl.BlockSpec((1, seq, LANES), lambda b, h: (b, 0, 0)),
                  pl.BlockSpec((1, FOX_HEADS, seq), lambda b, h: (b, 0, 0))],
        out_specs=pl.BlockSpec((seq, HEAD_DIM), lambda b, h: (b, h)),
        out_shape=jax.ShapeDtypeStruct((tokens, D_MODEL), jnp.bfloat16),
        compiler_params=_params("parallel", "parallel"),
        name="fox_attention",
    )(qkv, qkv, qkv, c, ct)


def _diff_attn_kernel(lam_ref, q1_ref, q2_ref, k1_ref, k2_ref, v_ref, cos_ref, sa_ref, sb_ref, g_ref,
                      o_ref, q1s, q2s, k1s, k2s, *, seq, out_scale):
    cos = cos_ref[...]
    sa = sa_ref[...]
    sb = sb_ref[...]
    half = ROT_DIM // 2
    for src, dst in ((q1_ref, q1s), (q2_ref, q2s), (k1_ref, k1s), (k2_ref, k2s)):
        x = src[...].astype(jnp.float32)
        rot = x * cos + pltpu.roll(x, half, axis=1) * sa + pltpu.roll(x, HEAD_DIM - half, axis=1) * sb
        dst[...] = rot.astype(jnp.bfloat16)
    lam = lam_ref[0]
    g = g_ref[...] * out_scale
    for i in range(seq // ATT_BLOCK):
        lo = i * ATT_BLOCK
        a1 = _causal_attend(q1s[lo:lo + ATT_BLOCK, :], k1s, v_ref, i, None, None)
        a2 = _causal_attend(q2s[lo:lo + ATT_BLOCK, :], k2s, v_ref, i, None, None)
        o = a1 - lam * a2
        o = o * lax.rsqrt(jnp.mean(jnp.square(o), axis=-1, keepdims=True) + SUBLN_EPS)
        o_ref[lo:lo + ATT_BLOCK, :] = (o * g).astype(o_ref.dtype)


def _diff_attention(qkv, lam, cos_t, sin_a, sin_b, subln_g, lam_init, batch, seq):
    tokens = batch * seq
    kern = functools.partial(_diff_attn_kernel, seq=seq, out_scale=1.0 - lam_init)
    kv_off = 2 * DIFF_HEADS
    v_off = 2 * DIFF_HEADS
    tab = pl.BlockSpec((seq, HEAD_DIM), lambda b, h: (0, 0))
    return pl.pallas_call(
        kern,
        grid=(batch, DIFF_HEADS),
        in_specs=[pl.BlockSpec(memory_space=pltpu.SMEM),
                  pl.BlockSpec((seq, HEAD_DIM), lambda b, h: (b, 2 * h)),
                  pl.BlockSpec((seq, HEAD_DIM), lambda b, h: (b, 2 * h + 1)),
                  pl.BlockSpec((seq, HEAD_DIM), lambda b, h: (b, kv_off + 2 * h)),
                  pl.BlockSpec((seq, HEAD_DIM), lambda b, h: (b, kv_off + 2 * h + 1)),
                  pl.BlockSpec((seq, 2 * HEAD_DIM), lambda b, h: (b, v_off + h)),
                  tab, tab, tab,
                  pl.BlockSpec((1, 2 * HEAD_DIM), lambda b, h: (0, 0))],
        out_specs=pl.BlockSpec((seq, 2 * HEAD_DIM), lambda b, h: (b, h)),
        out_shape=jax.ShapeDtypeStruct((tokens, D_MODEL), jnp.bfloat16),
        scratch_shapes=[pltpu.VMEM((seq, HEAD_DIM), jnp.bfloat16)] * 4,
        compiler_params=_params("parallel", "parallel"),
        name="diff_attention",
    )(lam, qkv, qkv, qkv, qkv, qkv, cos_t, sin_a, sin_b, subln_g)


def _deepnorm(z, g, b):
    mu = jnp.mean(z, axis=-1, keepdims=True)
    zc = z - mu
    var = jnp.mean(zc * zc, axis=-1, keepdims=True)
    return zc * lax.rsqrt(var + LN_EPS) * g + b


def _proj_ln_kernel(a_ref, w_ref, h_ref, g_ref, b_ref, of_ref, ob_ref, ol_ref, *, tm):
    y = jnp.dot(a_ref[...], w_ref[...], preferred_element_type=jnp.float32)
    out = _deepnorm(DEEPNORM_ALPHA * h_ref[...] + y, g_ref[...], b_ref[...])
    of_ref[...] = out
    ob_ref[...] = out.astype(jnp.bfloat16)
    _pack_rows(ol_ref, 0, tm, out)


def _proj_ln(a, w, h, g, b):
    tokens = a.shape[0]
    tm = 256
    row = pl.BlockSpec((tm, D_MODEL), lambda i: (i, 0))
    vec = pl.BlockSpec((1, D_MODEL), lambda i: (0, 0))
    return pl.pallas_call(
        functools.partial(_proj_ln_kernel, tm=tm),
        grid=(tokens // tm,),
        in_specs=[row, pl.BlockSpec((D_MODEL, D_MODEL), lambda i: (0, 0)), row, vec, vec],
        out_specs=[row, row, pl.BlockSpec((tm * ROW_CHUNKS, LANES), lambda i: (i, 0))],
        out_shape=[jax.ShapeDtypeStruct((tokens, D_MODEL), jnp.float32),
                   jax.ShapeDtypeStruct((tokens, D_MODEL), jnp.bfloat16),
                   jax.ShapeDtypeStruct((tokens * ROW_CHUNKS, LANES), jnp.uint32)],
        compiler_params=_params("parallel"),
        name="outproj_deepnorm",
    )(a, w, h, g, b)


def _router_kernel(h_ref, wt_ref, bias_ref, idx_ref, w_ref, rank_ref, cnt_ref, carry_ref, *, tm):
    step = pl.program_id(0)

    @pl.when(step == 0)
    def _():
        carry_ref[...] = jnp.zeros_like(carry_ref)

    logits = lax.dot_general(wt_ref[...], h_ref[...], _NT, precision=_HIGHEST,
                             preferred_element_type=jnp.float32)
    scores = 1.0 / (1.0 + jnp.exp(-logits))
    biased = scores + bias_ref[...]
    npg = EXPERTS_PER_GROUP
    group = lax.broadcasted_iota(jnp.int32, (N_GROUPS, tm), 0)
    s = [scores[j * N_GROUPS:(j + 1) * N_GROUPS] for j in range(npg)]
    bz = [biased[j * N_GROUPS:(j + 1) * N_GROUPS] for j in range(npg)]
    eid = [group * npg + j for j in range(npg)]
    neg = -jnp.inf

    m1 = functools.reduce(jnp.maximum, bz)
    j1 = functools.reduce(jnp.minimum, [jnp.where(bz[j] == m1, j, npg) for j in range(npg)])
    m2 = functools.reduce(jnp.maximum, [jnp.where(j1 == j, neg, bz[j]) for j in range(npg)])
    gs = m1 + m2
    beaten = jnp.zeros((N_GROUPS, tm), jnp.int32)
    for r in range(1, N_GROUPS):
        og = pltpu.roll(gs, r, axis=0)
        oi = pltpu.roll(group, r, axis=0)
        wins = jnp.where(og > gs, 1, jnp.where(og == gs, jnp.where(oi < group, 1, 0), 0))
        beaten = beaten + wins
    keep = beaten < TOPK_GROUPS
    mb = [jnp.where(keep, bz[j], neg) for j in range(npg)]

    sel_idx, sel_w = [], []
    chosen = [jnp.zeros((N_GROUPS, tm), jnp.float32) for _ in range(npg)]
    for _ in range(TOP_K):
        m = jnp.max(functools.reduce(jnp.maximum, mb), axis=0, keepdims=True)
        cand = functools.reduce(jnp.minimum, [jnp.where(mb[j] == m, eid[j], N_EXPERTS) for j in range(npg)])
        ik = jnp.min(cand, axis=0, keepdims=True)
        hit = [eid[j] == ik for j in range(npg)]
        wk = functools.reduce(jnp.add, [jnp.where(hit[j], s[j], 0.0) for j in range(npg)])
        sel_idx.append(ik)
        sel_w.append(jnp.sum(wk, axis=0, keepdims=True))
        mb = [jnp.where(hit[j], neg, mb[j]) for j in range(npg)]
        chosen = [jnp.where(hit[j], 1.0, chosen[j]) for j in range(npg)]
    wsum = functools.reduce(jnp.add, sel_w)
    idx_ref[...] = jnp.concatenate(sel_idx, axis=0)
    w_ref[...] = jnp.concatenate(sel_w, axis=0) / wsum * ROUTED_SCALE

    onehot = jnp.concatenate(chosen, axis=0)
    tr = lax.broadcasted_iota(jnp.int32, (tm, tm), 0)
    tc = lax.broadcasted_iota(jnp.int32, (tm, tm), 1)
    before = (tr < tc).astype(jnp.bfloat16)
    excl = jnp.dot(onehot.astype(jnp.bfloat16), before, preferred_element_type=jnp.float32)
    base = excl + carry_ref[:, 0:1]
    bs = [base[j * N_GROUPS:(j + 1) * N_GROUPS] for j in range(npg)]
    ranks = []
    for k in range(TOP_K):
        rk = functools.reduce(jnp.add, [jnp.where(eid[j] == sel_idx[k], bs[j], 0.0) for j in range(npg)])
        ranks.append(jnp.sum(rk, axis=0, keepdims=True))
    rank_ref[...] = jnp.concatenate(ranks, axis=0).astype(jnp.int32)
    carry_ref[...] = carry_ref[...] + jnp.sum(onehot, axis=1, keepdims=True)
    cnt_ref[...] = carry_ref[...]


def _router(h, wt_perm, bias_perm):
    tokens = h.shape[0]
    tm = 512
    sel = pl.BlockSpec((TOP_K, tm), lambda i: (0, i))
    return pl.pallas_call(
        functools.partial(_router_kernel, tm=tm),
        grid=(tokens // tm,),
        in_specs=[pl.BlockSpec((tm, D_MODEL), lambda i: (i, 0)),
                  pl.BlockSpec((N_EXPERTS, D_MODEL), lambda i: (0, 0)),
                  pl.BlockSpec((N_EXPERTS, 1), lambda i: (0, 0))],
        out_specs=[sel, sel, sel, pl.BlockSpec((N_EXPERTS, LANES), lambda i: (0, 0))],
        out_shape=[jax.ShapeDtypeStruct((TOP_K, tokens), jnp.int32),
                   jax.ShapeDtypeStruct((TOP_K, tokens), jnp.float32),
                   jax.ShapeDtypeStruct((TOP_K, tokens), jnp.int32),
                   jax.ShapeDtypeStruct((N_EXPERTS, LANES), jnp.float32)],
        scratch_shapes=[pltpu.VMEM((N_EXPERTS, LANES), jnp.float32)],
        compiler_params=_params("arbitrary"),
        name="moe_router",
    )(h, wt_perm, bias_perm)


def _dispatch_kernel(fill_ref, pos_ref, h3_ref, xs_ref, zbuf, sem, zsem, *, tt):
    step = pl.program_id(0)

    @pl.when(step == 0)
    def _():
        zbuf[...] = jnp.zeros_like(zbuf)
        n_blocks = fill_ref.shape[0]

        def block(j):
            first_row = pl.multiple_of(j * EXPERT_BLOCK, EXPERT_BLOCK)
            return pltpu.make_async_copy(zbuf, xs_ref.at[pl.ds(first_row, EXPERT_BLOCK)], zsem)

        def fill(j, carry):
            @pl.when(fill_ref[j] != 0)
            def _():
                block(j).start()

            return carry

        def drain(j, carry):
            @pl.when(fill_ref[j] != 0)
            def _():
                block(j).wait()

            return carry

        lax.fori_loop(0, n_blocks, fill, 0)
        lax.fori_loop(0, n_blocks, drain, 0)

    def issue(t, carry):
        for k in range(TOP_K):
            pltpu.make_async_copy(h3_ref.at[t], xs_ref.at[pos_ref[k, t]], sem).start(priority=k % 2)
        return carry

    lax.fori_loop(0, tt, issue, 0)
    rows = pl.ds(0, tt * TOP_K)
    pltpu.make_async_copy(xs_ref.at[rows], xs_ref.at[rows], sem).wait()


def _dispatch(fill_blocks, pos, h_lin, n_rows):
    tokens = h_lin.shape[0] // ROW_CHUNKS
    tt = 256
    h3 = h_lin.reshape(tokens, ROW_CHUNKS, LANES)
    xs = pl.pallas_call(
        functools.partial(_dispatch_kernel, tt=tt),
        grid_spec=pltpu.PrefetchScalarGridSpec(
            num_scalar_prefetch=1,
            grid=(tokens // tt,),
            in_specs=[pl.BlockSpec((TOP_K, tt), lambda i, fb: (0, i), memory_space=pltpu.SMEM),
                      pl.BlockSpec((tt, ROW_CHUNKS, LANES), lambda i, fb: (i, 0, 0))],
            out_specs=pl.BlockSpec(memory_space=pl.ANY),
            scratch_shapes=[pltpu.VMEM((EXPERT_BLOCK, ROW_CHUNKS, LANES), jnp.uint32),
                            pltpu.SemaphoreType.DMA, pltpu.SemaphoreType.DMA]),
        out_shape=jax.ShapeDtypeStruct((n_rows, ROW_CHUNKS, LANES), jnp.uint32),
        compiler_params=_params("arbitrary"),
        name="moe_dispatch",
    )(fill_blocks, pos, h3)
    return xs.reshape(n_rows * ROW_CHUNKS, LANES)


def _silu_mul(gate_up):
    half = gate_up.shape[-1] // 2
    gate = gate_up[:, :half]
    up = gate_up[:, half:]
    return (gate / (1.0 + jnp.exp(-gate)) * up).astype(jnp.bfloat16)


def _expert_kernel(be_ref, nxt_ref, nu_ref, x_ref, wgu_hbm, wd_hbm, y_ref, wgu_f, wd_f, wgu_b, wd_b,
                   sem_gu, sem_d):
    b = pl.program_id(0)
    active = b < nu_ref[0]
    expert = be_ref[b]
    prev = be_ref[jnp.maximum(b - 1, 0)]

    def weight_copies(e):
        return (pltpu.make_async_copy(wgu_hbm.at[e], wgu_f, sem_gu),
                pltpu.make_async_copy(wd_hbm.at[e], wd_f, sem_d))

    @pl.when(b == 0)
    def _():
        for copy in weight_copies(expert):
            copy.start()

    @pl.when(active & ((b == 0) | (expert != prev)))
    def _():
        for copy in weight_copies(expert):
            copy.wait()
        wgu_b[...] = wgu_f[...].astype(jnp.bfloat16)
        wd_b[...] = wd_f[...].astype(jnp.bfloat16)
        following = nxt_ref[b]

        @pl.when(following != expert)
        def _():
            for copy in weight_copies(following):
                copy.start()

    @pl.when(active)
    def _():
        chunks = [_unpack_chunk(x_ref, 0, EXPERT_BLOCK, s) for s in range(ROW_CHUNKS)]
        x = jnp.concatenate([lo for lo, _ in chunks] + [hi for _, hi in chunks],
                            axis=-1).astype(jnp.bfloat16)
        act = _silu_mul(jnp.dot(x, wgu_b[...], preferred_element_type=jnp.float32))
        y = jnp.dot(act, wd_b[...], preferred_element_type=jnp.float32)
        _pack_rows(y_ref, 0, EXPERT_BLOCK, y)

    @pl.when(jnp.logical_not(active))
    def _():
        y_ref[...] = jnp.zeros_like(y_ref)


def _experts(block_expert, block_next, n_used, x_lin, wgu, wd):
    n_blocks = block_expert.shape[0]
    rows = EXPERT_BLOCK * ROW_CHUNKS
    return pl.pallas_call(
        _expert_kernel,
        grid_spec=pltpu.PrefetchScalarGridSpec(
            num_scalar_prefetch=3,
            grid=(n_blocks,),
            in_specs=[pl.BlockSpec((rows, LANES), lambda b, be, nx, nu: (b, 0)),
                      pl.BlockSpec(memory_space=pl.ANY),
                      pl.BlockSpec(memory_space=pl.ANY)],
            out_specs=pl.BlockSpec((rows, LANES), lambda b, be, nx, nu: (b, 0)),
            scratch_shapes=[pltpu.VMEM((D_MODEL, 2 * EXPERT_DIM), jnp.float32),
                            pltpu.VMEM((EXPERT_DIM, D_MODEL), jnp.float32),
                            pltpu.VMEM((D_MODEL, 2 * EXPERT_DIM), jnp.bfloat16),
                            pltpu.VMEM((EXPERT_DIM, D_MODEL), jnp.bfloat16),
                            pltpu.SemaphoreType.DMA, pltpu.SemaphoreType.DMA]),
        out_shape=jax.ShapeDtypeStruct(x_lin.shape, jnp.uint32),
        compiler_params=_params("arbitrary"),
        name="moe_experts",
    )(block_expert, block_next, n_used, x_lin, wgu, wd)


def _moe_out_kernel(pos_ref, posn_ref, hb_ref, h_ref, ys_ref, w_ref, wsgu_ref, wsd_ref, g_ref, b_ref,
                    of_ref, ob_ref, buf0, buf1, sem0, sem1, *, tm):
    step = pl.program_id(0)
    nstep = pl.num_programs(0)

    def start_row(p_ref, col, t, k, buf, sem):
        dst = (k * tm + t) * ROW_CHUNKS
        if not isinstance(dst, int):
            dst = pl.multiple_of(dst, ROW_CHUNKS)
        pltpu.make_async_copy(ys_ref.at[p_ref[k, col]], buf.at[pl.ds(dst, ROW_CHUNKS), :], sem).start(priority=k % 2)

    def gather(p_ref, off, buf, sem):
        for t in range(tm):
            for k in range(TOP_K):
                start_row(p_ref, off + t, t, k, buf, sem)

    def gather_rolled(p_ref, off, buf, sem):
        def body(t, carry):
            for k in range(TOP_K):
                start_row(p_ref, off + t, t, k, buf, sem)
            return carry

        lax.fori_loop(0, tm, body, 0)

    def wait_all(buf, sem):
        pltpu.make_async_copy(buf, buf, sem).wait()

    def compute(buf, r0):
        rows = slice(r0, r0 + tm)
        act = _silu_mul(jnp.dot(hb_ref[rows, :], wsgu_ref[...], preferred_element_type=jnp.float32))
        y = jnp.dot(act, wsd_ref[...], preferred_element_type=jnp.float32)
        w = w_ref[rows, :]
        wk = [jnp.broadcast_to(w[:, k:k + 1], (tm, LANES)) for k in range(TOP_K)]
        acc_lo, acc_hi = [], []
        for s in range(ROW_CHUNKS):
            a_lo = jnp.zeros((tm, LANES), jnp.float32)
            a_hi = jnp.zeros((tm, LANES), jnp.float32)
            for k in range(TOP_K):
                lo, hi = _unpack_chunk(buf, k * tm, tm, s)
                a_lo = a_lo + wk[k] * lo
                a_hi = a_hi + wk[k] * hi
            acc_lo.append(a_lo)
            acc_hi.append(a_hi)
        y = y + jnp.concatenate(acc_lo + acc_hi, axis=-1)
        out = _deepnorm(DEEPNORM_ALPHA * h_ref[rows, :] + y, g_ref[...], b_ref[...])
        of_ref[rows, :] = out
        ob_ref[rows, :] = out.astype(jnp.bfloat16)

    @pl.when(step == 0)
    def _():
        gather_rolled(pos_ref, 0, buf0, sem0)

    gather(pos_ref, tm, buf1, sem1)
    wait_all(buf0, sem0)
    compute(buf0, 0)
    gather(posn_ref, 0, buf0, sem0)
    wait_all(buf1, sem1)
    compute(buf1, tm)

    @pl.when(step == nstep - 1)
    def _():
        wait_all(buf0, sem0)


def _moe_out(pos, hb, h, y_lin, w_tk, wsgu, wsd, g, b):
    tokens = h.shape[0]
    tm = 128
    nstep = tokens // (2 * tm)
    n_rows = y_lin.shape[0] // ROW_CHUNKS
    row = pl.BlockSpec((2 * tm, D_MODEL), lambda i: (i, 0))
    vec = pl.BlockSpec((1, D_MODEL), lambda i: (0, 0))
    return pl.pallas_call(
        functools.partial(_moe_out_kernel, tm=tm),
        grid=(nstep,),
        in_specs=[pl.BlockSpec((TOP_K, 2 * tm), lambda i: (0, i), memory_space=pltpu.SMEM),
                  pl.BlockSpec((TOP_K, 2 * tm), lambda i: (0, jnp.minimum(i + 1, nstep - 1)),
                               memory_space=pltpu.SMEM),
                  row, row,
                  pl.BlockSpec(memory_space=pl.ANY),
                  pl.BlockSpec((2 * tm, TOP_K), lambda i: (i, 0)),
                  pl.BlockSpec((D_MODEL, 2 * EXPERT_DIM), lambda i: (0, 0)),
                  pl.BlockSpec((EXPERT_DIM, D_MODEL), lambda i: (0, 0)),
                  vec, vec],
        out_specs=[row, row],
        out_shape=[jax.ShapeDtypeStruct((tokens, D_MODEL), jnp.float32),
                   jax.ShapeDtypeStruct((tokens, D_MODEL), jnp.bfloat16)],
        scratch_shapes=[pltpu.VMEM((TOP_K * tm * ROW_CHUNKS, LANES), jnp.uint32),
                        pltpu.VMEM((TOP_K * tm * ROW_CHUNKS, LANES), jnp.uint32),
                        pltpu.SemaphoreType.DMA, pltpu.SemaphoreType.DMA],
        compiler_params=_params("arbitrary"),
        name="moe_combine_deepnorm",
    )(pos, pos, hb, h, y_lin.reshape(n_rows, ROW_CHUNKS, LANES), w_tk, wsgu, wsd, g, b)


def _moe_layer(h, hb, h_lin, w_router, b_router, w_gate_up, w_down, ws_gate_up, ws_down, g, b):
    tokens = h.shape[0]
    n_assign = tokens * TOP_K
    n_blocks = n_assign // EXPERT_BLOCK + N_EXPERTS
    n_rows = n_blocks * EXPERT_BLOCK
    perm = (jnp.arange(N_EXPERTS) % N_GROUPS) * EXPERTS_PER_GROUP + jnp.arange(N_EXPERTS) // N_GROUPS
    top_idx, top_w, rank, cnt_perm = _router(h, w_router.T[perm], b_router[perm].reshape(N_EXPERTS, 1))

    counts = jnp.zeros((N_EXPERTS,), jnp.int32).at[perm].set(cnt_perm[:, 0].astype(jnp.int32))
    padded = (counts + EXPERT_BLOCK - 1) // EXPERT_BLOCK * EXPERT_BLOCK
    padded_end = jnp.cumsum(padded)
    padded_start = padded_end - padded
    experts = jnp.arange(N_EXPERTS, dtype=jnp.int32)[:, None, None]
    pos = rank + jnp.sum(jnp.where(top_idx[None] == experts, padded_start[:, None, None], 0), axis=0)
    block_row = jnp.arange(n_blocks, dtype=jnp.int32) * EXPERT_BLOCK
    block_expert = jnp.minimum(jnp.sum((padded_end[None, :] <= block_row[:, None]).astype(jnp.int32), axis=1),
                               N_EXPERTS - 1)
    n_used = (padded_end[-1:] // EXPERT_BLOCK).astype(jnp.int32)
    ids = jnp.arange(N_EXPERTS, dtype=jnp.int32)
    later = jnp.where((counts > 0)[None, :] & (ids[None, :] > ids[:, None]), ids[None, :], N_EXPERTS)
    following = jnp.min(later, axis=1)
    expert_next = jnp.where(following < N_EXPERTS, following, ids)
    block_next = jnp.sum(jnp.where(block_expert[:, None] == ids[None, :], expert_next[None, :], 0), axis=1)
    block_ids = jnp.arange(n_blocks, dtype=jnp.int32)
    expert_ends = jnp.concatenate([block_expert[1:] != block_expert[:-1], jnp.ones((1,), jnp.bool_)])
    fill_blocks = (expert_ends | (block_ids >= n_used[0] - 1)).astype(jnp.int32)

    x_lin = _dispatch(fill_blocks, pos, h_lin, n_rows)
    y_lin = _experts(block_expert, block_next.astype(jnp.int32), n_used, x_lin, w_gate_up, w_down)
    return _moe_out(pos, hb, h, y_lin, top_w.T, ws_gate_up.astype(jnp.bfloat16), ws_down.astype(jnp.bfloat16),
                    g.reshape(1, D_MODEL), b.reshape(1, D_MODEL))


def _q_scale(width, q_cols):
    return jnp.where(jnp.arange(width) < q_cols, HEAD_DIM ** -0.5 * LOG2E, 1.0).astype(jnp.float32).reshape(1, width)


def _fox_mixer(hb, w_in, b_forget, batch, seq):
    width = 3 * D_MODEL
    qkv = _matmul(hb, w_in[:, :width].astype(jnp.bfloat16), _q_scale(width, D_MODEL), jnp.bfloat16,
                  1024, 1024, "fox_in_proj")
    w_f = jnp.pad(w_in[:, width:], ((0, 0), (0, LANES - FOX_HEADS))).astype(jnp.bfloat16)
    f_logit = _matmul(hb, w_f, jnp.ones((1, LANES), jnp.float32), jnp.float32, 1024, LANES, "fox_gate_proj")
    b_pad = jnp.pad(b_forget, (0, LANES - FOX_HEADS)).reshape(1, LANES)
    c, ct = _forget_cumsum(f_logit, b_pad, batch, seq)
    return _fox_attention(qkv, c, ct, batch, seq)


def _rotary_tables(seq):
    half = ROT_DIM // 2
    inv_freq = ROPE_THETA ** (-jnp.arange(0, ROT_DIM, 2, dtype=jnp.float32) / ROT_DIM)
    ang = jnp.arange(seq, dtype=jnp.float32)[:, None] * inv_freq[None, :]
    cos, sin = jnp.cos(ang), jnp.sin(ang)
    zeros = jnp.zeros((seq, HEAD_DIM - ROT_DIM), jnp.float32)
    zh = jnp.zeros((seq, half), jnp.float32)
    cos_t = jnp.concatenate([cos, cos, jnp.ones_like(zeros)], axis=1)
    sin_a = jnp.concatenate([zh, sin, zeros], axis=1)
    sin_b = jnp.concatenate([-sin, zh, zeros], axis=1)
    return cos_t, sin_a, sin_b


def _diff_mixer(hb, layer_idx, w_in, lambda_qk, subln_g, tables, batch, seq):
    width = 3 * D_MODEL
    qkv = _matmul(hb, w_in.astype(jnp.bfloat16), _q_scale(width, D_MODEL), jnp.bfloat16, 1024, 1024,
                  "diff_in_proj")
    lam_init = 0.8 - 0.6 * math.exp(-0.3 * layer_idx)
    lq = lambda_qk.astype(jnp.float32)
    lam = (jnp.exp(jnp.sum(lq[0] * lq[1])) - jnp.exp(jnp.sum(lq[2] * lq[3])) + lam_init).reshape(1)
    return _diff_attention(qkv, lam, *tables, subln_g.reshape(1, 2 * HEAD_DIM), lam_init, batch, seq)


def kernel(x, fox_w_in_0, fox_b_forget_0, fox_w_out_0, ln_mix_g_0, ln_mix_b_0, moe_w_router_0, moe_b_router_0, moe_w_gate_up_0, moe_w_down_0, moe_ws_gate_up_0, moe_ws_down_0, ln_ffn_g_0, ln_ffn_b_0, diff_w_in_1, diff_lambda_qk_1, diff_subln_g_1, diff_w_out_1, ln_mix_g_1, ln_mix_b_1, moe_w_router_1, moe_b_router_1, moe_w_gate_up_1, moe_w_down_1, moe_ws_gate_up_1, moe_ws_down_1, ln_ffn_g_1, ln_ffn_b_1, fox_w_in_2, fox_b_forget_2, fox_w_out_2, ln_mix_g_2, ln_mix_b_2, moe_w_router_2, moe_b_router_2, moe_w_gate_up_2, moe_w_down_2, moe_ws_gate_up_2, moe_ws_down_2, ln_ffn_g_2, ln_ffn_b_2, diff_w_in_3, diff_lambda_qk_3, diff_subln_g_3, diff_w_out_3, ln_mix_g_3, ln_mix_b_3, moe_w_router_3, moe_b_router_3, moe_w_gate_up_3, moe_w_down_3, moe_ws_gate_up_3, moe_ws_down_3, ln_ffn_g_3, ln_ffn_b_3):
    batch, seq, _ = x.shape
    tokens = batch * seq
    mix = [(fox_w_in_0, fox_b_forget_0, fox_w_out_0),
           (diff_w_in_1, diff_lambda_qk_1, diff_subln_g_1, diff_w_out_1),
           (fox_w_in_2, fox_b_forget_2, fox_w_out_2),
           (diff_w_in_3, diff_lambda_qk_3, diff_subln_g_3, diff_w_out_3)]
    norm_mix = [(ln_mix_g_0, ln_mix_b_0), (ln_mix_g_1, ln_mix_b_1), (ln_mix_g_2, ln_mix_b_2), (ln_mix_g_3, ln_mix_b_3)]
    moe = [(moe_w_router_0, moe_b_router_0, moe_w_gate_up_0, moe_w_down_0, moe_ws_gate_up_0, moe_ws_down_0),
           (moe_w_router_1, moe_b_router_1, moe_w_gate_up_1, moe_w_down_1, moe_ws_gate_up_1, moe_ws_down_1),
           (moe_w_router_2, moe_b_router_2, moe_w_gate_up_2, moe_w_down_2, moe_ws_gate_up_2, moe_ws_down_2),
           (moe_w_router_3, moe_b_router_3, moe_w_gate_up_3, moe_w_down_3, moe_ws_gate_up_3, moe_ws_down_3)]
    norm_ffn = [(ln_ffn_g_0, ln_ffn_b_0), (ln_ffn_g_1, ln_ffn_b_1), (ln_ffn_g_2, ln_ffn_b_2), (ln_ffn_g_3, ln_ffn_b_3)]

    tables = _rotary_tables(seq)
    h = x.reshape(tokens, D_MODEL)
    hb = h.astype(jnp.bfloat16)
    for i in range(DEPTH):
        if i % 2 == 0:
            w_in, b_forget, w_out = mix[i]
            o = _fox_mixer(hb, w_in, b_forget, batch, seq)
        else:
            w_in, lambda_qk, subln_g, w_out = mix[i]
            o = _diff_mixer(hb, i, w_in, lambda_qk, subln_g, tables, batch, seq)
        g, b = norm_mix[i]
        h, hb, h_lin = _proj_ln(o, w_out.astype(jnp.bfloat16), h, g.reshape(1, D_MODEL), b.reshape(1, D_MODEL))
        h, hb = _moe_layer(h, hb, h_lin, *moe[i], *norm_ffn[i])
    return h.reshape(batch, seq, D_MODEL)
```

```python
import functools
import math

import jax
import jax.numpy as jnp
from jax import lax
from jax.experimental import pallas as pl
from jax.experimental.pallas import tpu as pltpu

D_MODEL = 2048
DEPTH = 4
HEAD_DIM = 128
LANES = 128
ROW_CHUNKS = D_MODEL // (2 * LANES)
FOX_HEADS = D_MODEL // HEAD_DIM
DIFF_HEADS = D_MODEL // (2 * HEAD_DIM)
ROT_DIM = HEAD_DIM // 4
ROPE_THETA = 500000.0
N_EXPERTS = 64
TOP_K = 8
N_GROUPS = 8
TOPK_GROUPS = 4
EXPERTS_PER_GROUP = N_EXPERTS // N_GROUPS
EXPERT_DIM = D_MODEL // 4
ROUTED_SCALE = 2.5
DEEPNORM_ALPHA = (2 * DEPTH) ** 0.25
LN_EPS = 1e-5
SUBLN_EPS = 1e-5
LOG2E = math.log2(math.e)

ATT_BLOCK = 256
EXPERT_BLOCK = 512
VMEM_LIMIT = 56 * 1024 * 1024

_HIGHEST = lax.Precision.HIGHEST
_NT = (((1,), (1,)), ((), ()))


def _params(*sem):
    return pltpu.CompilerParams(dimension_semantics=sem, vmem_limit_bytes=VMEM_LIMIT)


def _pack_rows(ref, row0, rows, x):
    half = D_MODEL // 2
    for s in range(ROW_CHUNKS):
        lo = x[:, s * LANES:(s + 1) * LANES].astype(jnp.bfloat16).astype(jnp.float32)
        hi = x[:, half + s * LANES:half + (s + 1) * LANES].astype(jnp.bfloat16).astype(jnp.float32)
        word = lax.bitcast_convert_type(hi, jnp.uint32) | (lax.bitcast_convert_type(lo, jnp.uint32) >> 16)
        ref[pl.ds(row0 * ROW_CHUNKS + s, rows, stride=ROW_CHUNKS), :] = word


def _unpack_chunk(ref, row0, rows, s):
    u = ref[pl.ds(row0 * ROW_CHUNKS + s, rows, stride=ROW_CHUNKS), :]
    lo = lax.bitcast_convert_type(u << 16, jnp.float32)
    hi = lax.bitcast_convert_type(u & jnp.uint32(0xFFFF0000), jnp.float32)
    return lo, hi


def _mm_kernel(a_ref, b_ref, s_ref, o_ref):
    acc = jnp.dot(a_ref[...], b_ref[...], preferred_element_type=jnp.float32)
    o_ref[...] = (acc * s_ref[...]).astype(o_ref.dtype)


def _matmul(a, b, col_scale, out_dtype, tm, tn, name):
    m, k = a.shape
    n = b.shape[1]
    return pl.pallas_call(
        _mm_kernel,
        grid=(m // tm, n // tn),
        in_specs=[pl.BlockSpec((tm, k), lambda i, j: (i, 0)),
                  pl.BlockSpec((k, tn), lambda i, j: (0, j)),
                  pl.BlockSpec((1, tn), lambda i, j: (0, j))],
        out_specs=pl.BlockSpec((tm, tn), lambda i, j: (i, j)),
        out_shape=jax.ShapeDtypeStruct((m, n), out_dtype),
        compiler_params=_params("parallel", "parallel"),
        name=name,
    )(a, b, col_scale)


def _fgate_kernel(fl_ref, b_ref, c_ref, ct_ref, *, seq, chunk):
    x = fl_ref[...] + b_ref[...]
    ls = -(jnp.maximum(-x, 0.0) + jnp.log1p(jnp.exp(-jnp.abs(x))))
    r = lax.broadcasted_iota(jnp.int32, (chunk, chunk), 0)
    c = lax.broadcasted_iota(jnp.int32, (chunk, chunk), 1)
    tri = (r >= c).astype(jnp.float32)
    carry = jnp.zeros((1, LANES), jnp.float32)
    for ch in range(seq // chunk):
        blk = jnp.dot(tri, ls[ch * chunk:(ch + 1) * chunk], precision=_HIGHEST,
                      preferred_element_type=jnp.float32) + carry
        carry = blk[chunk - 1:chunk, :]
        c_ref[0, ch * chunk:(ch + 1) * chunk, :] = blk
        ct_ref[0, :, ch * chunk:(ch + 1) * chunk] = blk.T[:FOX_HEADS]


def _forget_cumsum(f_logit, b_pad, batch, seq):
    chunk = 256
    return pl.pallas_call(
        functools.partial(_fgate_kernel, seq=seq, chunk=chunk),
        grid=(batch,),
        in_specs=[pl.BlockSpec((seq, LANES), lambda b: (b, 0)),
                  pl.BlockSpec((1, LANES), lambda b: (0, 0))],
        out_specs=[pl.BlockSpec((1, seq, LANES), lambda b: (b, 0, 0)),
                   pl.BlockSpec((1, FOX_HEADS, seq), lambda b: (b, 0, 0))],
        out_shape=[jax.ShapeDtypeStruct((batch, seq, LANES), jnp.float32),
                   jax.ShapeDtypeStruct((batch, FOX_HEADS, seq), jnp.float32)],
        compiler_params=_params("parallel"),
        name="forget_cumsum",
    )(f_logit, b_pad)


def _causal_attend(q, k_ref, v_ref, i, row_bias, col_bias):
    blk = ATT_BLOCK
    lo = i * blk
    s_d = lax.dot_general(q, k_ref[lo:lo + blk, :], _NT, preferred_element_type=jnp.float32)
    if col_bias is not None:
        s_d = s_d - col_bias[:, lo:lo + blk]
    r = lax.broadcasted_iota(jnp.int32, (blk, blk), 0)
    c = lax.broadcasted_iota(jnp.int32, (blk, blk), 1)
    s_d = jnp.where(c <= r, s_d, -jnp.inf)
    m = jnp.max(s_d, axis=-1, keepdims=True)
    if i > 0:
        s_o = lax.dot_general(q, k_ref[0:lo, :], _NT, preferred_element_type=jnp.float32)
        if col_bias is not None:
            s_o = s_o - col_bias[:, 0:lo]
        m = jnp.maximum(m, jnp.max(s_o, axis=-1, keepdims=True))
    if row_bias is not None:
        shift = row_bias - (row_bias + m)
    else:
        shift = -m
    p_d = jnp.exp2(s_d + shift)
    l = jnp.sum(p_d, axis=-1, keepdims=True)
    acc = jnp.dot(p_d.astype(jnp.bfloat16), v_ref[lo:lo + blk, :], preferred_element_type=jnp.float32)
    if i > 0:
        p_o = jnp.exp2(s_o + shift)
        l = l + jnp.sum(p_o, axis=-1, keepdims=True)
        acc = acc + jnp.dot(p_o.astype(jnp.bfloat16), v_ref[0:lo, :], preferred_element_type=jnp.float32)
    return acc / l


def _fox_attn_kernel(q_ref, k_ref, v_ref, c_ref, ct_ref, o_ref, *, seq):
    h = pl.program_id(1)
    c_all = c_ref[0]
    lane = lax.broadcasted_iota(jnp.int32, c_all.shape, 1)
    c_row = jnp.sum(jnp.where(lane == h, c_all, 0.0), axis=1, keepdims=True) * LOG2E
    ct_all = ct_ref[0]
    sub = lax.broadcasted_iota(jnp.int32, ct_all.shape, 0)
    c_col = jnp.sum(jnp.where(sub == h, ct_all, 0.0), axis=0, keepdims=True) * LOG2E
    for i in range(seq // ATT_BLOCK):
        lo = i * ATT_BLOCK
        q = q_ref[lo:lo + ATT_BLOCK, :]
        out = _causal_attend(q, k_ref, v_ref, i, c_row[lo:lo + ATT_BLOCK, :], c_col)
        o_ref[lo:lo + ATT_BLOCK, :] = out.astype(o_ref.dtype)


def _fox_attention(qkv, c, ct, batch, seq):
    tokens = batch * seq
    kern = functools.partial(_fox_attn_kernel, seq=seq)
    return pl.pallas_call(
        kern,
        grid=(batch, FOX_HEADS),
        in_specs=[pl.BlockSpec((seq, HEAD_DIM), lambda b, h: (b, h)),
                  pl.BlockSpec((seq, HEAD_DIM), lambda b, h: (b, FOX_HEADS + h)),
                  pl.BlockSpec((seq, HEAD_DIM), lambda b, h: (b, 2 * FOX_HEADS + h)),
                  pl.BlockSpec((1, seq, LANES), lambda b, h: (b, 0, 0)),
                  pl.BlockSpec((1, FOX_HEADS, seq), lambda b, h: (b, 0, 0))],
        out_specs=pl.BlockSpec((seq, HEAD_DIM), lambda b, h: (b, h)),
        out_shape=jax.ShapeDtypeStruct((tokens, D_MODEL), jnp.bfloat16),
        compiler_params=_params("parallel", "parallel"),
        name="fox_attention",
    )(qkv, qkv, qkv, c, ct)


def _diff_attn_kernel(lam_ref, q1_ref, q2_ref, k1_ref, k2_ref, v_ref, cos_ref, sa_ref, sb_ref, g_ref,
                      o_ref, q1s, q2s, k1s, k2s, *, seq, out_scale):
    cos = cos_ref[...]
    sa = sa_ref[...]
    sb = sb_ref[...]
    half = ROT_DIM // 2
    for src, dst in ((q1_ref, q1s), (q2_ref, q2s), (k1_ref, k1s), (k2_ref, k2s)):
        x = src[...].astype(jnp.float32)
        rot = x * cos + pltpu.roll(x, half, axis=1) * sa + pltpu.roll(x, HEAD_DIM - half, axis=1) * sb
        dst[...] = rot.astype(jnp.bfloat16)
    lam = lam_ref[0]
    g = g_ref[...] * out_scale
    for i in range(seq // ATT_BLOCK):
        lo = i * ATT_BLOCK
        a1 = _causal_attend(q1s[lo:lo + ATT_BLOCK, :], k1s, v_ref, i, None, None)
        a2 = _causal_attend(q2s[lo:lo + ATT_BLOCK, :], k2s, v_ref, i, None, None)
        o = a1 - lam * a2
        o = o * lax.rsqrt(jnp.mean(jnp.square(o), axis=-1, keepdims=True) + SUBLN_EPS)
        o_ref[lo:lo + ATT_BLOCK, :] = (o * g).astype(o_ref.dtype)


def _diff_attention(qkv, lam, cos_t, sin_a, sin_b, subln_g, lam_init, batch, seq):
    tokens = batch * seq
    kern = functools.partial(_diff_attn_kernel, seq=seq, out_scale=1.0 - lam_init)
    kv_off = 2 * DIFF_HEADS
    v_off = 2 * DIFF_HEADS
    tab = pl.BlockSpec((seq, HEAD_DIM), lambda b, h: (0, 0))
    return pl.pallas_call(
        kern,
        grid=(batch, DIFF_HEADS),
        in_specs=[pl.BlockSpec(memory_space=pltpu.SMEM),
                  pl.BlockSpec((seq, HEAD_DIM), lambda b, h: (b, 2 * h)),
                  pl.BlockSpec((seq, HEAD_DIM), lambda b, h: (b, 2 * h + 1)),
                  pl.BlockSpec((seq, HEAD_DIM), lambda b, h: (b, kv_off + 2 * h)),
                  pl.BlockSpec((seq, HEAD_DIM), lambda b, h: (b, kv_off + 2 * h + 1)),
                  pl.BlockSpec((seq, 2 * HEAD_DIM), lambda b, h: (b, v_off + h)),
                  tab, tab, tab,
                  pl.BlockSpec((1, 2 * HEAD_DIM), lambda b, h: (0, 0))],
        out_specs=pl.BlockSpec((seq, 2 * HEAD_DIM), lambda b, h: (b, h)),
        out_shape=jax.ShapeDtypeStruct((tokens, D_MODEL), jnp.bfloat16),
        scratch_shapes=[pltpu.VMEM((seq, HEAD_DIM), jnp.bfloat16)] * 4,
        compiler_params=_params("parallel", "parallel"),
        name="diff_attention",
    )(lam, qkv, qkv, qkv, qkv, qkv, cos_t, sin_a, sin_b, subln_g)


def _deepnorm(z, g, b):
    mu = jnp.mean(z, axis=-1, keepdims=True)
    zc = z - mu
    var = jnp.mean(zc * zc, axis=-1, keepdims=True)
    return zc * lax.rsqrt(var + LN_EPS) * g + b


def _proj_ln_kernel(a_ref, w_ref, h_ref, g_ref, b_ref, of_ref, ob_ref, ol_ref, *, tm):
    y = jnp.dot(a_ref[...], w_ref[...], preferred_element_type=jnp.float32)
    out = _deepnorm(DEEPNORM_ALPHA * h_ref[...] + y, g_ref[...], b_ref[...])
    of_ref[...] = out
    ob_ref[...] = out.astype(jnp.bfloat16)
    _pack_rows(ol_ref, 0, tm, out)


def _proj_ln(a, w, h, g, b):
    tokens = a.shape[0]
    tm = 256
    row = pl.BlockSpec((tm, D_MODEL), lambda i: (i, 0))
    vec = pl.BlockSpec((1, D_MODEL), lambda i: (0, 0))
    return pl.pallas_call(
        functools.partial(_proj_ln_kernel, tm=tm),
        grid=(tokens // tm,),
        in_specs=[row, pl.BlockSpec((D_MODEL, D_MODEL), lambda i: (0, 0)), row, vec, vec],
        out_specs=[row, row, pl.BlockSpec((tm * ROW_CHUNKS, LANES), lambda i: (i, 0))],
        out_shape=[jax.ShapeDtypeStruct((tokens, D_MODEL), jnp.float32),
                   jax.ShapeDtypeStruct((tokens, D_MODEL), jnp.bfloat16),
                   jax.ShapeDtypeStruct((tokens * ROW_CHUNKS, LANES), jnp.uint32)],
        compiler_params=_params("parallel"),
        name="outproj_deepnorm",
    )(a, w, h, g, b)


def _router_kernel(h_ref, wt_ref, bias_ref, idx_ref, w_ref, rank_ref, cnt_ref, carry_ref, *, tm):
    step = pl.program_id(0)

    @pl.when(step == 0)
    def _():
        carry_ref[...] = jnp.zeros_like(carry_ref)

    logits = lax.dot_general(wt_ref[...], h_ref[...], _NT, precision=_HIGHEST,
                             preferred_element_type=jnp.float32)
    scores = 1.0 / (1.0 + jnp.exp(-logits))
    biased = scores + bias_ref[...]
    npg = EXPERTS_PER_GROUP
    group = lax.broadcasted_iota(jnp.int32, (N_GROUPS, tm), 0)
    s = [scores[j * N_GROUPS:(j + 1) * N_GROUPS] for j in range(npg)]
    bz = [biased[j * N_GROUPS:(j + 1) * N_GROUPS] for j in range(npg)]
    eid = [group * npg + j for j in range(npg)]
    neg = -jnp.inf

    m1 = functools.reduce(jnp.maximum, bz)
    j1 = functools.reduce(jnp.minimum, [jnp.where(bz[j] == m1, j, npg) for j in range(npg)])
    m2 = functools.reduce(jnp.maximum, [jnp.where(j1 == j, neg, bz[j]) for j in range(npg)])
    gs = m1 + m2
    beaten = jnp.zeros((N_GROUPS, tm), jnp.int32)
    for r in range(1, N_GROUPS):
        og = pltpu.roll(gs, r, axis=0)
        oi = pltpu.roll(group, r, axis=0)
        wins = jnp.where(og > gs, 1, jnp.where(og == gs, jnp.where(oi < group, 1, 0), 0))
        beaten = beaten + wins
    keep = beaten < TOPK_GROUPS
    mb = [jnp.where(keep, bz[j], neg) for j in range(npg)]

    sel_idx, sel_w = [], []
    chosen = [jnp.zeros((N_GROUPS, tm), jnp.float32) for _ in range(npg)]
    for _ in range(TOP_K):
        m = jnp.max(functools.reduce(jnp.maximum, mb), axis=0, keepdims=True)
        cand = functools.reduce(jnp.minimum, [jnp.where(mb[j] == m, eid[j], N_EXPERTS) for j in range(npg)])
        ik = jnp.min(cand, axis=0, keepdims=True)
        hit = [eid[j] == ik for j in range(npg)]
        wk = functools.reduce(jnp.add, [jnp.where(hit[j], s[j], 0.0) for j in range(npg)])
        sel_idx.append(ik)
        sel_w.append(jnp.sum(wk, axis=0, keepdims=True))
        mb = [jnp.where(hit[j], neg, mb[j]) for j in range(npg)]
        chosen = [jnp.where(hit[j], 1.0, chosen[j]) for j in range(npg)]
    wsum = functools.reduce(jnp.add, sel_w)
    idx_ref[...] = jnp.concatenate(sel_idx, axis=0)
    w_ref[...] = jnp.concatenate(sel_w, axis=0) / wsum * ROUTED_SCALE

    onehot = jnp.concatenate(chosen, axis=0)
    tr = lax.broadcasted_iota(jnp.int32, (tm, tm), 0)
    tc = lax.broadcasted_iota(jnp.int32, (tm, tm), 1)
    before = (tr < tc).astype(jnp.bfloat16)
    excl = jnp.dot(onehot.astype(jnp.bfloat16), before, preferred_element_type=jnp.float32)
    base = excl + carry_ref[:, 0:1]
    bs = [base[j * N_GROUPS:(j + 1) * N_GROUPS] for j in range(npg)]
    ranks = []
    for k in range(TOP_K):
        rk = functools.reduce(jnp.add, [jnp.where(eid[j] == sel_idx[k], bs[j], 0.0) for j in range(npg)])
        ranks.append(jnp.sum(rk, axis=0, keepdims=True))
    rank_ref[...] = jnp.concatenate(ranks, axis=0).astype(jnp.int32)
    carry_ref[...] = carry_ref[...] + jnp.sum(onehot, axis=1, keepdims=True)
    cnt_ref[...] = carry_ref[...]


def _router(h, wt_perm, bias_perm):
    tokens = h.shape[0]
    tm = 512
    sel = pl.BlockSpec((TOP_K, tm), lambda i: (0, i))
    return pl.pallas_call(
        functools.partial(_router_kernel, tm=tm),
        grid=(tokens // tm,),
        in_specs=[pl.BlockSpec((tm, D_MODEL), lambda i: (i, 0)),
                  pl.BlockSpec((N_EXPERTS, D_MODEL), lambda i: (0, 0)),
                  pl.BlockSpec((N_EXPERTS, 1), lambda i: (0, 0))],
        out_specs=[sel, sel, sel, pl.BlockSpec((N_EXPERTS, LANES), lambda i: (0, 0))],
        out_shape=[jax.ShapeDtypeStruct((TOP_K, tokens), jnp.int32),
                   jax.ShapeDtypeStruct((TOP_K, tokens), jnp.float32),
                   jax.ShapeDtypeStruct((TOP_K, tokens), jnp.int32),
                   jax.ShapeDtypeStruct((N_EXPERTS, LANES), jnp.float32)],
        scratch_shapes=[pltpu.VMEM((N_EXPERTS, LANES), jnp.float32)],
        compiler_params=_params("arbitrary"),
        name="moe_router",
    )(h, wt_perm, bias_perm)


def _dispatch_kernel(fill_ref, pos_ref, h3_ref, xs_ref, zbuf, sem, zsem, *, tt):
    step = pl.program_id(0)

    @pl.when(step == 0)
    def _():
        zbuf[...] = jnp.zeros_like(zbuf)
        n_blocks = fill_ref.shape[0]

        def block(j):
            first_row = pl.multiple_of(j * EXPERT_BLOCK, EXPERT_BLOCK)
            return pltpu.make_async_copy(zbuf, xs_ref.at[pl.ds(first_row, EXPERT_BLOCK)], zsem)

        def fill(j, carry):
            @pl.when(fill_ref[j] != 0)
            def _():
                block(j).start()

            return carry

        def drain(j, carry):
            @pl.when(fill_ref[j] != 0)
            def _():
                block(j).wait()

            return carry

        lax.fori_loop(0, n_blocks, fill, 0)
        lax.fori_loop(0, n_blocks, drain, 0)

    def issue(t, carry):
        for k in range(TOP_K):
            pltpu.make_async_copy(h3_ref.at[t], xs_ref.at[pos_ref[k, t]], sem).start(priority=k % 2)
        return carry

    lax.fori_loop(0, tt, issue, 0)
    rows = pl.ds(0, tt * TOP_K)
    pltpu.make_async_copy(xs_ref.at[rows], xs_ref.at[rows], sem).wait()


def _dispatch(fill_blocks, pos, h_lin, n_rows):
    tokens = h_lin.shape[0] // ROW_CHUNKS
    tt = 512
    h3 = h_lin.reshape(tokens, ROW_CHUNKS, LANES)
    xs = pl.pallas_call(
        functools.partial(_dispatch_kernel, tt=tt),
        grid_spec=pltpu.PrefetchScalarGridSpec(
            num_scalar_prefetch=1,
            grid=(tokens // tt,),
            in_specs=[pl.BlockSpec((TOP_K, tt), lambda i, fb: (0, i), memory_space=pltpu.SMEM),
                      pl.BlockSpec((tt, ROW_CHUNKS, LANES), lambda i, fb: (i, 0, 0))],
            out_specs=pl.BlockSpec(memory_space=pl.ANY),
            scratch_shapes=[pltpu.VMEM((EXPERT_BLOCK, ROW_CHUNKS, LANES), jnp.uint32),
                            pltpu.SemaphoreType.DMA, pltpu.SemaphoreType.DMA]),
        out_shape=jax.ShapeDtypeStruct((n_rows, ROW_CHUNKS, LANES), jnp.uint32),
        compiler_params=_params("arbitrary"),
        name="moe_dispatch",
    )(fill_blocks, pos, h3)
    return xs.reshape(n_rows * ROW_CHUNKS, LANES)


def _silu_mul(gate_up):
    half = gate_up.shape[-1] // 2
    gate = gate_up[:, :half]
    up = gate_up[:, half:]
    return (gate / (1.0 + jnp.exp(-gate)) * up).astype(jnp.bfloat16)


def _expert_kernel(be_ref, nxt_ref, nu_ref, x_ref, wgu_hbm, wd_hbm, y_ref, wgu_f, wd_f, wgu_b, wd_b,
                   sem_gu, sem_d):
    b = pl.program_id(0)
    active = b < nu_ref[0]
    expert = be_ref[b]
    prev = be_ref[jnp.maximum(b - 1, 0)]

    def weight_copies(e):
        return (pltpu.make_async_copy(wgu_hbm.at[e], wgu_f, sem_gu),
                pltpu.make_async_copy(wd_hbm.at[e], wd_f, sem_d))

    @pl.when(b == 0)
    def _():
        for copy in weight_copies(expert):
            copy.start()

    @pl.when(active & ((b == 0) | (expert != prev)))
    def _():
        for copy in weight_copies(expert):
            copy.wait()
        wgu_b[...] = wgu_f[...].astype(jnp.bfloat16)
        wd_b[...] = wd_f[...].astype(jnp.bfloat16)
        following = nxt_ref[b]

        @pl.when(following != expert)
        def _():
            for copy in weight_copies(following):
                copy.start()

    @pl.when(active)
    def _():
        chunks = [_unpack_chunk(x_ref, 0, EXPERT_BLOCK, s) for s in range(ROW_CHUNKS)]
        x = jnp.concatenate([lo for lo, _ in chunks] + [hi for _, hi in chunks],
                            axis=-1).astype(jnp.bfloat16)
        act = _silu_mul(jnp.dot(x, wgu_b[...], preferred_element_type=jnp.float32))
        y = jnp.dot(act, wd_b[...], preferred_element_type=jnp.float32)
        _pack_rows(y_ref, 0, EXPERT_BLOCK, y)

    @pl.when(jnp.logical_not(active))
    def _():
        y_ref[...] = jnp.zeros_like(y_ref)


def _experts(block_expert, block_next, n_used, x_lin, wgu, wd):
    n_blocks = block_expert.shape[0]
    rows = EXPERT_BLOCK * ROW_CHUNKS
    return pl.pallas_call(
        _expert_kernel,
        grid_spec=pltpu.PrefetchScalarGridSpec(
            num_scalar_prefetch=3,
            grid=(n_blocks,),
            in_specs=[pl.BlockSpec((rows, LANES), lambda b, be, nx, nu: (b, 0)),
                      pl.BlockSpec(memory_space=pl.ANY),
                      pl.BlockSpec(memory_space=pl.ANY)],
            out_specs=pl.BlockSpec((rows, LANES), lambda b, be, nx, nu: (b, 0)),
            scratch_shapes=[pltpu.VMEM((D_MODEL, 2 * EXPERT_DIM), jnp.float32),
                            pltpu.VMEM((EXPERT_DIM, D_MODEL), jnp.float32),
                            pltpu.VMEM((D_MODEL, 2 * EXPERT_DIM), jnp.bfloat16),
                            pltpu.VMEM((EXPERT_DIM, D_MODEL), jnp.bfloat16),
                            pltpu.SemaphoreType.DMA, pltpu.SemaphoreType.DMA]),
        out_shape=jax.ShapeDtypeStruct(x_lin.shape, jnp.uint32),
        compiler_params=_params("arbitrary"),
        name="moe_experts",
    )(block_expert, block_next, n_used, x_lin, wgu, wd)


def _moe_out_kernel(pos_ref, posn_ref, hb_ref, h_ref, ys_ref, w_ref, wsgu_ref, wsd_ref, g_ref, b_ref,
                    of_ref, ob_ref, buf0, buf1, sem0, sem1, *, tm):
    step = pl.program_id(0)
    nstep = pl.num_programs(0)

    def start_row(p_ref, col, t, k, buf, sem):
        dst = (k * tm + t) * ROW_CHUNKS
        if not isinstance(dst, int):
            dst = pl.multiple_of(dst, ROW_CHUNKS)
        pltpu.make_async_copy(ys_ref.at[p_ref[k, col]], buf.at[pl.ds(dst, ROW_CHUNKS), :], sem).start(priority=k % 2)

    def gather(p_ref, off, buf, sem):
        for t in range(tm):
            for k in range(TOP_K):
                start_row(p_ref, off + t, t, k, buf, sem)

    def gather_rolled(p_ref, off, buf, sem):
        def body(t, carry):
            for k in range(TOP_K):
                start_row(p_ref, off + t, t, k, buf, sem)
            return carry

        lax.fori_loop(0, tm, body, 0)

    def wait_all(buf, sem):
        pltpu.make_async_copy(buf, buf, sem).wait()

    def compute(buf, r0):
        rows = slice(r0, r0 + tm)
        act = _silu_mul(jnp.dot(hb_ref[rows, :], wsgu_ref[...], preferred_element_type=jnp.float32))
        y = jnp.dot(act, wsd_ref[...], preferred_element_type=jnp.float32)
        w = w_ref[rows, :]
        wk = [jnp.broadcast_to(w[:, k:k + 1], (tm, LANES)) for k in range(TOP_K)]
        acc_lo, acc_hi = [], []
        for s in range(ROW_CHUNKS):
            a_lo = jnp.zeros((tm, LANES), jnp.float32)
            a_hi = jnp.zeros((tm, LANES), jnp.float32)
            for k in range(TOP_K):
                lo, hi = _unpack_chunk(buf, k * tm, tm, s)
                a_lo = a_lo + wk[k] * lo
                a_hi = a_hi + wk[k] * hi
            acc_lo.append(a_lo)
            acc_hi.append(a_hi)
        y = y + jnp.concatenate(acc_lo + acc_hi, axis=-1)
        out = _deepnorm(DEEPNORM_ALPHA * h_ref[rows, :] + y, g_ref[...], b_ref[...])
        of_ref[rows, :] = out
        ob_ref[rows, :] = out.astype(jnp.bfloat16)

    @pl.when(step == 0)
    def _():
        gather_rolled(pos_ref, 0, buf0, sem0)

    gather(pos_ref, tm, buf1, sem1)
    wait_all(buf0, sem0)
    compute(buf0, 0)
    gather(posn_ref, 0, buf0, sem0)
    wait_all(buf1, sem1)
    compute(buf1, tm)

    @pl.when(step == nstep - 1)
    def _():
        wait_all(buf0, sem0)


def _moe_out(pos, hb, h, y_lin, w_tk, wsgu, wsd, g, b):
    tokens = h.shape[0]
    tm = 128
    nstep = tokens // (2 * tm)
    n_rows = y_lin.shape[0] // ROW_CHUNKS
    row = pl.BlockSpec((2 * tm, D_MODEL), lambda i: (i, 0))
    vec = pl.BlockSpec((1, D_MODEL), lambda i: (0, 0))
    return pl.pallas_call(
        functools.partial(_moe_out_kernel, tm=tm),
        grid=(nstep,),
        in_specs=[pl.BlockSpec((TOP_K, 2 * tm), lambda i: (0, i), memory_space=pltpu.SMEM),
                  pl.BlockSpec((TOP_K, 2 * tm), lambda i: (0, jnp.minimum(i + 1, nstep - 1)),
                               memory_space=pltpu.SMEM),
                  row, row,
                  pl.BlockSpec(memory_space=pl.ANY),
                  pl.BlockSpec((2 * tm, TOP_K), lambda i: (i, 0)),
                  pl.BlockSpec((D_MODEL, 2 * EXPERT_DIM), lambda i: (0, 0)),
                  pl.BlockSpec((EXPERT_DIM, D_MODEL), lambda i: (0, 0)),
                  vec, vec],
        out_specs=[row, row],
        out_shape=[jax.ShapeDtypeStruct((tokens, D_MODEL), jnp.float32),
                   jax.ShapeDtypeStruct((tokens, D_MODEL), jnp.bfloat16)],
        scratch_shapes=[pltpu.VMEM((TOP_K * tm * ROW_CHUNKS, LANES), jnp.uint32),
                        pltpu.VMEM((TOP_K * tm * ROW_CHUNKS, LANES), jnp.uint32),
                        pltpu.SemaphoreType.DMA, pltpu.SemaphoreType.DMA],
        compiler_params=_params("arbitrary"),
        name="moe_combine_deepnorm",
    )(pos, pos, hb, h, y_lin.reshape(n_rows, ROW_CHUNKS, LANES), w_tk, wsgu, wsd, g, b)


def _moe_layer(h, hb, h_lin, w_router, b_router, w_gate_up, w_down, ws_gate_up, ws_down, g, b):
    tokens = h.shape[0]
    n_assign = tokens * TOP_K
    n_blocks = n_assign // EXPERT_BLOCK + N_EXPERTS
    n_rows = n_blocks * EXPERT_BLOCK
    perm = (jnp.arange(N_EXPERTS) % N_GROUPS) * EXPERTS_PER_GROUP + jnp.arange(N_EXPERTS) // N_GROUPS
    top_idx, top_w, rank, cnt_perm = _router(h, w_router.T[perm], b_router[perm].reshape(N_EXPERTS, 1))

    counts = jnp.zeros((N_EXPERTS,), jnp.int32).at[perm].set(cnt_perm[:, 0].astype(jnp.int32))
    padded = (counts + EXPERT_BLOCK - 1) // EXPERT_BLOCK * EXPERT_BLOCK
    padded_end = jnp.cumsum(padded)
    padded_start = padded_end - padded
    experts = jnp.arange(N_EXPERTS, dtype=jnp.int32)[:, None, None]
    pos = rank + jnp.sum(jnp.where(top_idx[None] == experts, padded_start[:, None, None], 0), axis=0)
    block_row = jnp.arange(n_blocks, dtype=jnp.int32) * EXPERT_BLOCK
    block_expert = jnp.minimum(jnp.sum((padded_end[None, :] <= block_row[:, None]).astype(jnp.int32), axis=1),
                               N_EXPERTS - 1)
    n_used = (padded_end[-1:] // EXPERT_BLOCK).astype(jnp.int32)
    ids = jnp.arange(N_EXPERTS, dtype=jnp.int32)
    later = jnp.where((counts > 0)[None, :] & (ids[None, :] > ids[:, None]), ids[None, :], N_EXPERTS)
    following = jnp.min(later, axis=1)
    expert_next = jnp.where(following < N_EXPERTS, following, ids)
    block_next = jnp.sum(jnp.where(block_expert[:, None] == ids[None, :], expert_next[None, :], 0), axis=1)
    block_ids = jnp.arange(n_blocks, dtype=jnp.int32)
    expert_ends = jnp.concatenate([block_expert[1:] != block_expert[:-1], jnp.ones((1,), jnp.bool_)])
    fill_blocks = (expert_ends | (block_ids >= n_used[0] - 1)).astype(jnp.int32)

    x_lin = _dispatch(fill_blocks, pos, h_lin, n_rows)
    y_lin = _experts(block_expert, block_next.astype(jnp.int32), n_used, x_lin, w_gate_up, w_down)
    return _moe_out(pos, hb, h, y_lin, top_w.T, ws_gate_up.astype(jnp.bfloat16), ws_down.astype(jnp.bfloat16),
                    g.reshape(1, D_MODEL), b.reshape(1, D_MODEL))


def _q_scale(width, q_cols):
    return jnp.where(jnp.arange(width) < q_cols, HEAD_DIM ** -0.5 * LOG2E, 1.0).astype(jnp.float32).reshape(1, width)


def _fox_mixer(hb, w_in, b_forget, batch, seq):
    width = 3 * D_MODEL
    qkv = _matmul(hb, w_in[:, :width].astype(jnp.bfloat16), _q_scale(width, D_MODEL), jnp.bfloat16,
                  2048, 1024, "fox_in_proj")
    w_f = jnp.pad(w_in[:, width:], ((0, 0), (0, LANES - FOX_HEADS))).astype(jnp.bfloat16)
    f_logit = _matmul(hb, w_f, jnp.ones((1, LANES), jnp.float32), jnp.float32, 1024, LANES, "fox_gate_proj")
    b_pad = jnp.pad(b_forget, (0, LANES - FOX_HEADS)).reshape(1, LANES)
    c, ct = _forget_cumsum(f_logit, b_pad, batch, seq)
    return _fox_attention(qkv, c, ct, batch, seq)


def _rotary_tables(seq):
    half = ROT_DIM // 2
    inv_freq = ROPE_THETA ** (-jnp.arange(0, ROT_DIM, 2, dtype=jnp.float32) / ROT_DIM)
    ang = jnp.arange(seq, dtype=jnp.float32)[:, None] * inv_freq[None, :]
    cos, sin = jnp.cos(ang), jnp.sin(ang)
    zeros = jnp.zeros((seq, HEAD_DIM - ROT_DIM), jnp.float32)
    zh = jnp.zeros((seq, half), jnp.float32)
    cos_t = jnp.concatenate([cos, cos, jnp.ones_like(zeros)], axis=1)
    sin_a = jnp.concatenate([zh, sin, zeros], axis=1)
    sin_b = jnp.concatenate([-sin, zh, zeros], axis=1)
    return cos_t, sin_a, sin_b


def _diff_mixer(hb, layer_idx, w_in, lambda_qk, subln_g, tables, batch, seq):
    width = 3 * D_MODEL
    qkv = _matmul(hb, w_in.astype(jnp.bfloat16), _q_scale(width, D_MODEL), jnp.bfloat16, 2048, 1024,
                  "diff_in_proj")
    lam_init = 0.8 - 0.6 * math.exp(-0.3 * layer_idx)
    lq = lambda_qk.astype(jnp.float32)
    lam = (jnp.exp(jnp.sum(lq[0] * lq[1])) - jnp.exp(jnp.sum(lq[2] * lq[3])) + lam_init).reshape(1)
    return _diff_attention(qkv, lam, *tables, subln_g.reshape(1, 2 * HEAD_DIM), lam_init, batch, seq)


def kernel(x, fox_w_in_0, fox_b_forget_0, fox_w_out_0, ln_mix_g_0, ln_mix_b_0, moe_w_router_0, moe_b_router_0, moe_w_gate_up_0, moe_w_down_0, moe_ws_gate_up_0, moe_ws_down_0, ln_ffn_g_0, ln_ffn_b_0, diff_w_in_1, diff_lambda_qk_1, diff_subln_g_1, diff_w_out_1, ln_mix_g_1, ln_mix_b_1, moe_w_router_1, moe_b_router_1, moe_w_gate_up_1, moe_w_down_1, moe_ws_gate_up_1, moe_ws_down_1, ln_ffn_g_1, ln_ffn_b_1, fox_w_in_2, fox_b_forget_2, fox_w_out_2, ln_mix_g_2, ln_mix_b_2, moe_w_router_2, moe_b_router_2, moe_w_gate_up_2, moe_w_down_2, moe_ws_gate_up_2, moe_ws_down_2, ln_ffn_g_2, ln_ffn_b_2, diff_w_in_3, diff_lambda_qk_3, diff_subln_g_3, diff_w_out_3, ln_mix_g_3, ln_mix_b_3, moe_w_router_3, moe_b_router_3, moe_w_gate_up_3, moe_w_down_3, moe_ws_gate_up_3, moe_ws_down_3, ln_ffn_g_3, ln_ffn_b_3):
    batch, seq, _ = x.shape
    tokens = batch * seq
    mix = [(fox_w_in_0, fox_b_forget_0, fox_w_out_0),
           (diff_w_in_1, diff_lambda_qk_1, diff_subln_g_1, diff_w_out_1),
           (fox_w_in_2, fox_b_forget_2, fox_w_out_2),
           (diff_w_in_3, diff_lambda_qk_3, diff_subln_g_3, diff_w_out_3)]
    norm_mix = [(ln_mix_g_0, ln_mix_b_0), (ln_mix_g_1, ln_mix_b_1), (ln_mix_g_2, ln_mix_b_2), (ln_mix_g_3, ln_mix_b_3)]
    moe = [(moe_w_router_0, moe_b_router_0, moe_w_gate_up_0, moe_w_down_0, moe_ws_gate_up_0, moe_ws_down_0),
           (moe_w_router_1, moe_b_router_1, moe_w_gate_up_1, moe_w_down_1, moe_ws_gate_up_1, moe_ws_down_1),
           (moe_w_router_2, moe_b_router_2, moe_w_gate_up_2, moe_w_down_2, moe_ws_gate_up_2, moe_ws_down_2),
           (moe_w_router_3, moe_b_router_3, moe_w_gate_up_3, moe_w_down_3, moe_ws_gate_up_3, moe_ws_down_3)]
    norm_ffn = [(ln_ffn_g_0, ln_ffn_b_0), (ln_ffn_g_1, ln_ffn_b_1), (ln_ffn_g_2, ln_ffn_b_2), (ln_ffn_g_3, ln_ffn_b_3)]

    tables = _rotary_tables(seq)
    h = x.reshape(tokens, D_MODEL)
    hb = h.astype(jnp.bfloat16)
    for i in range(DEPTH):
        if i % 2 == 0:
            w_in, b_forget, w_out = mix[i]
            o = _fox_mixer(hb, w_in, b_forget, batch, seq)
        else:
            w_in, lambda_qk, subln_g, w_out = mix[i]
            o = _diff_mixer(hb, i, w_in, lambda_qk, subln_g, tables, batch, seq)
        g, b = norm_mix[i]
        h, hb, h_lin = _proj_ln(o, w_out.astype(jnp.bfloat16), h, g.reshape(1, D_MODEL), b.reshape(1, D_MODEL))
        h, hb = _moe_layer(h, hb, h_lin, *moe[i], *norm_ffn[i])
    return h.reshape(batch, seq, D_MODEL)
```
